```python
import jax, jax.numpy as jnp
from jax import lax
import numpy as np

D_MODEL = 1024
BATCH = 16
SEQ = 4096
DEPTH = 4

N_EVEN = (DEPTH + 1) // 2
N_ODD = DEPTH // 2
MIX_WIDTH = D_MODEL
LRU_WIDTH = MIX_WIDTH // 2
LRU_HEADS = 8
LRU_HEAD_DIM = LRU_WIDTH // LRU_HEADS
CONV_WIDTH = 4
LRU_C = 8.0
HGRN_WIDTH = MIX_WIDTH // 2
HGRN_HEAD_DIM = 128
HGRN_HEADS = HGRN_WIDTH // HGRN_HEAD_DIM
HGRN_CHUNK = 64
SGU_WIDTH = MIX_WIDTH // 2
SGU_GROUPS = 4
SGU_GROUP_DIM = SGU_WIDTH // SGU_GROUPS
SGU_CHUNK = 128
POOL_WIDTH = MIX_WIDTH // 2
POOL_WINDOWS = (2, 4, 8, 16)
POOL_GROUP_DIM = POOL_WIDTH // len(POOL_WINDOWS)
D_FF = 4 * D_MODEL
EVEN_SPLITS = (LRU_WIDTH, LRU_WIDTH, HGRN_WIDTH, HGRN_WIDTH, HGRN_WIDTH, HGRN_WIDTH)
EVEN_IN = sum(EVEN_SPLITS)
ODD_IN = 2 * SGU_WIDTH + POOL_WIDTH
EPS = 1e-6

kernel_name = "hybrid_rglru_hgrn2_sgu_pool_trunk"

F32 = jnp.float32


def rms_norm(x, gain):
    xf = x.astype(F32)
    y = xf * lax.rsqrt(jnp.mean(xf * xf, axis=-1, keepdims=True) + EPS)
    return (y * gain.astype(F32)).astype(x.dtype)


def causal_depthwise_conv(u, w, b):
    s = u.shape[1]
    up = jnp.pad(u.astype(F32), ((0, 0), (CONV_WIDTH - 1, 0), (0, 0)))
    y = b.astype(F32)
    for k in range(CONV_WIDTH):
        y = y + up[:, k:k + s] * w[k].astype(F32)
    return y


def rg_lru(u, w_a, b_a, w_x, b_x, lam):
    bsz, s, w = u.shape
    uh = u.reshape(bsz, s, LRU_HEADS, LRU_HEAD_DIM)
    r = jax.nn.sigmoid(jnp.einsum('bshi,hij->bshj', uh, w_a.astype(F32)).reshape(bsz, s, w) + b_a.astype(F32))
    i = jax.nn.sigmoid(jnp.einsum('bshi,hij->bshj', uh, w_x.astype(F32)).reshape(bsz, s, w) + b_x.astype(F32))
    log_a = -LRU_C * r * jax.nn.softplus(-lam.astype(F32))
    a = jnp.exp(log_a)
    beta = jnp.sqrt(jnp.maximum(-jnp.expm1(2.0 * log_a), 0.0))
    b_term = beta * (i * u)

    def combine(left, right):
        a_l, b_l = left
        a_r, b_r = right
        return a_l * a_r, a_r * b_l + b_r

    _, h = lax.associative_scan(combine, (a, b_term), axis=1)
    return h


def hgrn2(q, f_pre, v, g, lb, norm_gain):
    bsz, s, _ = q.shape
    nh, d, c = HGRN_HEADS, HGRN_HEAD_DIM, HGRN_CHUNK
    nc = s // c
    qf = jax.nn.silu(q.astype(F32))
    z = f_pre.astype(F32)
    lb = lb.astype(F32)
    f = lb + (1.0 - lb) * jax.nn.sigmoid(z)
    log_f = jnp.log(f)
    k = (1.0 - lb) * jax.nn.sigmoid(-z)
    vf = v.astype(F32)

    def to_chunks(t):
        return t.reshape(bsz, nc, c, nh, d).transpose(1, 0, 3, 2, 4)

    causal = jnp.tril(jnp.ones((c, c), dtype=bool))[:, :, None]
    causal_f = causal.astype(F32)

    def step(state, xs):
        qc, kc, vc, lfc = xs
        bcum = jnp.cumsum(lfc, axis=2)
        o_inter = jnp.einsum('bhtd,bhde->bhte', qc * jnp.exp(bcum), state)
        diff = bcum[:, :, :, None, :] - bcum[:, :, None, :, :]
        decay = jnp.exp(jnp.where(causal, diff, 0.0)) * causal_f
        attn = jnp.einsum('bhtd,bhsd,bhtsd->bhts', qc, kc, decay)
        o = o_inter + jnp.einsum('bhts,bhse->bhte', attn, vc)
        b_last = bcum[:, :, -1:, :]
        k_dec = kc * jnp.exp(b_last - bcum)
        new_state = jnp.exp(b_last[:, :, 0, :])[..., None] * state + jnp.einsum('bhsd,bhse->bhde', k_dec, vc)
        return new_state, o

    state0 = jnp.zeros((bsz, nh, d, d), F32)
    _, o = lax.scan(step, state0, (to_chunks(qf), to_chunks(k), to_chunks(vf), to_chunks(log_f)))
    o = o.transpose(1, 0, 3, 2, 4).reshape(bsz, s, nh, d)
    o = o * lax.rsqrt(jnp.mean(o * o, axis=-1, keepdims=True) + EPS)
    return o.reshape(bsz, s, nh * d) * norm_gain.astype(F32) * jax.nn.silu(g.astype(F32))


def spatial_gating(z, sgu_w, sgu_b):
    bsz, s, _ = z.shape
    n = s // SGU_CHUNK
    u = z[..., :SGU_WIDTH].astype(F32)
    v = z[..., SGU_WIDTH:].astype(F32).reshape(bsz, n, SGU_CHUNK, SGU_GROUPS, SGU_GROUP_DIM)
    mu = jnp.mean(v, axis=-1, keepdims=True)
    var = jnp.mean(jnp.square(v - mu), axis=-1, keepdims=True)
    vn = (v - mu) * lax.rsqrt(var + EPS)
    w = sgu_w.astype(F32) * jnp.tril(jnp.ones((SGU_CHUNK, SGU_CHUNK), F32))
    sv = jnp.einsum('gts,bnsgc->bntgc', w, vn) + sgu_b.astype(F32).T[None, None, :, :, None]
    return u * sv.reshape(bsz, s, SGU_WIDTH)


def multiscale_pool(p, pool_w, pool_scale):
    s = p.shape[1]
    pf = p.astype(F32)
    pos = jnp.arange(1, s + 1)
    outs = []
    for gi, win in enumerate(POOL_WINDOWS):
        pg = pf[..., gi * POOL_GROUP_DIM:(gi + 1) * POOL_GROUP_DIM]
        cs = jnp.cumsum(pg, axis=1)
        prev = jnp.pad(cs, ((0, 0), (win, 0), (0, 0)))[:, :s]
        count = jnp.minimum(pos, win).astype(F32)[None, :, None]
        pooled = (cs - prev) / count - pg
        outs.append(jnp.einsum('bsc,cd->bsd', pooled, pool_w[gi].astype(F32)))
    return jnp.concatenate(outs, axis=-1) * pool_scale.astype(F32)


def setup_inputs(seed: int = 0) -> dict:
    key = jax.random.key(seed)
    ks = jax.random.split(key, 24)
    nrm = lambda k, shape, scale: jax.random.normal(k, shape, F32) * scale
    a_init = jax.random.uniform(ks[9], (N_EVEN, LRU_WIDTH), F32, minval=0.9, maxval=0.999)
    s_init = a_init ** (1.0 / LRU_C)
    lam = jnp.log(s_init) - jnp.log1p(-s_init)
    return {
        "x": nrm(ks[0], (BATCH, SEQ, D_MODEL), 1.0),
        "norm_mix": 1.0 + nrm(ks[1], (DEPTH, D_MODEL), 0.02),
        "norm_ffn": 1.0 + nrm(ks[2], (DEPTH, D_MODEL), 0.02),
        "w_in_even": nrm(ks[3], (N_EVEN, D_MODEL, EVEN_IN), D_MODEL ** -0.5),
        "conv_w": nrm(ks[4], (N_EVEN, CONV_WIDTH, LRU_WIDTH), CONV_WIDTH ** -0.5),
        "conv_b": nrm(ks[5], (N_EVEN, LRU_WIDTH), 0.01),
        "lru_wa": nrm(ks[6], (N_EVEN, LRU_HEADS, LRU_HEAD_DIM, LRU_HEAD_DIM), LRU_HEAD_DIM ** -0.5),
        "lru_ba": nrm(ks[7], (N_EVEN, LRU_WIDTH), 0.01),
        "lru_wx": nrm(ks[8], (N_EVEN, LRU_HEADS, LRU_HEAD_DIM, LRU_HEAD_DIM), LRU_HEAD_DIM ** -0.5),
        "lru_bx": nrm(ks[10], (N_EVEN, LRU_WIDTH), 0.01),
        "lru_lambda": lam,
        "hgrn_lb_logits": nrm(ks[11], (N_EVEN, HGRN_WIDTH), 1.0),
        "hgrn_norm": 1.0 + nrm(ks[12], (N_EVEN, HGRN_WIDTH), 0.02),
        "w_out_even": nrm(ks[13], (N_EVEN, MIX_WIDTH, D_MODEL), MIX_WIDTH ** -0.5),
        "w_in_odd": nrm(ks[14], (N_ODD, D_MODEL, ODD_IN), D_MODEL ** -0.5),
        "sgu_w": nrm(ks[15], (N_ODD, SGU_GROUPS, SGU_CHUNK, SGU_CHUNK), SGU_CHUNK ** -0.5),
        "sgu_b": 1.0 + nrm(ks[16], (N_ODD, SGU_GROUPS, SGU_CHUNK), 0.01),
        "pool_w": nrm(ks[17], (N_ODD, len(POOL_WINDOWS), POOL_GROUP_DIM, POOL_GROUP_DIM), POOL_GROUP_DIM ** -0.5),
        "pool_scale": 1.0 + nrm(ks[18], (N_ODD, POOL_WIDTH), 0.02),
        "w_out_odd": nrm(ks[19], (N_ODD, MIX_WIDTH, D_MODEL), MIX_WIDTH ** -0.5),
        "w_ffn_in": nrm(ks[20], (DEPTH, D_MODEL, D_FF), D_MODEL ** -0.5),
        "w_ffn_out": nrm(ks[21], (DEPTH, D_FF, D_MODEL), D_FF ** -0.5),
        "norm_final": 1.0 + nrm(ks[22], (D_MODEL,), 0.02),
    }


def reference(x, norm_mix, norm_ffn, w_in_even, conv_w, conv_b, lru_wa, lru_ba, lru_wx, lru_bx,
              lru_lambda, hgrn_lb_logits, hgrn_norm, w_out_even, w_in_odd, sgu_w, sgu_b, pool_w,
              pool_scale, w_out_odd, w_ffn_in, w_ffn_out, norm_final):
    dt = x.dtype
    p_lb = jax.nn.softmax(hgrn_lb_logits.astype(F32), axis=0)
    lower_bounds = jnp.cumsum(p_lb, axis=0) - p_lb[0]
    even_cuts = [int(c) for c in np.cumsum(EVEN_SPLITS)[:-1]]
    h = x
    for layer in range(DEPTH):
        j = layer // 2
        xn = rms_norm(h, norm_mix[layer])
        if layer % 2 == 0:
            proj = xn @ w_in_even[j]
            xa, ga, q, f_pre, vi, gb = jnp.split(proj, even_cuts, axis=-1)
            ua = causal_depthwise_conv(xa, conv_w[j], conv_b[j])
            ya = rg_lru(ua, lru_wa[j], lru_ba[j], lru_wx[j], lru_bx[j], lru_lambda[j]) * jax.nn.gelu(ga.astype(F32))
            yb = hgrn2(q, f_pre, vi, gb, lower_bounds[j], hgrn_norm[j])
            mix = jnp.concatenate([ya, yb], axis=-1).astype(dt) @ w_out_even[j]
        else:
            proj = xn @ w_in_odd[j]
            zc = jax.nn.gelu(proj[..., :2 * SGU_WIDTH])
            pd = proj[..., 2 * SGU_WIDTH:]
            yc = spatial_gating(zc, sgu_w[j], sgu_b[j])
            yd = multiscale_pool(pd, pool_w[j], pool_scale[j])
            mix = jnp.concatenate([yc, yd], axis=-1).astype(dt) @ w_out_odd[j]
        h = h + mix
        hn = rms_norm(h, norm_ffn[layer])
        hidden = jnp.square(jax.nn.relu(hn @ w_ffn_in[layer]))
        h = h + hidden @ w_ffn_out[layer]
    return rms_norm(h, norm_final)
```

```python
import functools

import jax
import jax.numpy as jnp
from jax import lax
from jax.experimental import pallas as pl
from jax.experimental.pallas import tpu as pltpu

F32 = jnp.float32
BF16 = jnp.bfloat16

EPS = 1e-6
LRU_C = 8.0
CONV_WIDTH = 4
LRU_HEADS = 8
LRU_GROUP = 256
HGRN_HEAD_DIM = 128
HGRN_CHUNK = 64
HGRN_SUB = 16
SGU_CHUNK = 128
SGU_GROUPS = 4
POOL_WINDOWS = (2, 4, 8, 16)
POOL_HALO = 16
SUBLANES = 8

MIX_BLOCK_T = 512
FFN_BLOCK_M = 512
FFN_CHUNK = 1024
VMEM_LIMIT_BYTES = 56 * 1024 * 1024


def _dot(a, b):
    return jnp.dot(a, b, preferred_element_type=F32)


def _dot_nt(a, b):
    return lax.dot_general(a, b, (((1,), (1,)), ((), ())), preferred_element_type=F32)


def _dot_tn(a, b):
    return lax.dot_general(a, b, (((0,), (0,)), ((), ())), preferred_element_type=F32)


def _rms_norm(x, gain):
    ms = jnp.mean(x * x, axis=-1, keepdims=True)
    return x * lax.rsqrt(ms + EPS) * gain


def _sigmoid(x):
    return 1.0 / (1.0 + jnp.exp(-x))


def _sigmoid_pair(x):
    e = jnp.exp(-jnp.abs(x))
    big = 1.0 / (1.0 + e)
    small = e * big
    pos = x >= 0
    return jnp.where(pos, big, small), jnp.where(pos, small, big)


def _silu(x):
    return x * _sigmoid(x)


def _gelu_tanh(x):
    c = 0.7978845608028654
    return 0.5 * x * (1.0 + jnp.tanh(c * (x + 0.044715 * (x * x * x))))


def _split3_bf16(x):
    hi = x.astype(BF16)
    r1 = x - hi.astype(F32)
    mid = r1.astype(BF16)
    lo = (r1 - mid.astype(F32)).astype(BF16)
    return hi, mid, lo


def _ffn_kernel(h_ref, gain_ref, w1_ref, w2_ref, gfin_ref, o_ref, *, final_norm):
    h = h_ref[...]
    hn = _rms_norm(h, gain_ref[...]).astype(BF16)
    acc = h
    d_ff = w1_ref.shape[1]
    for c in range(d_ff // FFN_CHUNK):
        cols = slice(c * FFN_CHUNK, (c + 1) * FFN_CHUNK)
        hid = _dot(hn, w1_ref[:, cols])
        hid = jnp.square(jnp.maximum(hid, 0.0)).astype(BF16)
        acc = acc + _dot(hid, w2_ref[cols, :])
    if final_norm:
        acc = _rms_norm(acc, gfin_ref[...])
    o_ref[...] = acc


def _const_spec(shape):
    nd = len(shape)
    return pl.BlockSpec(shape, lambda *_: (0,) * nd, pipeline_mode=pl.Buffered(1))


def _ffn_call(h2d, gain, w1, w2, gfin, final_norm):
    m, d = h2d.shape
    d_ff = w1.shape[1]
    return pl.pallas_call(
        functools.partial(_ffn_kernel, final_norm=final_norm),
        grid=(m // FFN_BLOCK_M,),
        in_specs=[
            pl.BlockSpec((FFN_BLOCK_M, d), lambda i: (i, 0)),
            _const_spec((1, d)),
            _const_spec((d, d_ff)),
            _const_spec((d_ff, d)),
            _const_spec((1, d)),
        ],
        out_specs=pl.BlockSpec((FFN_BLOCK_M, d), lambda i: (i, 0)),
        out_shape=jax.ShapeDtypeStruct((m, d), F32),
        compiler_params=pltpu.CompilerParams(
            dimension_semantics=("parallel",), vmem_limit_bytes=VMEM_LIMIT_BYTES),
        name="ffn_final" if final_norm else "ffn",
    )(h2d, gain, w1, w2, gfin)


def _lru_scan(a, b, carry):
    t, w = a.shape
    g = t // SUBLANES
    a3 = a.reshape(g, SUBLANES, w)
    b3 = b.reshape(g, SUBLANES, w)
    row = lax.broadcasted_iota(jnp.int32, (g, SUBLANES, w), 1)
    sh = 1
    while sh < SUBLANES:
        keep = row >= sh
        a_prev = jnp.where(keep, pltpu.roll(a3, sh, 1), 1.0)
        b_prev = jnp.where(keep, pltpu.roll(b3, sh, 1), 0.0)
        b3 = a3 * b_prev + b3
        a3 = a3 * a_prev
        sh *= 2
    outs = []
    for i in range(g):
        hblk = a3[i] * carry + b3[i]
        outs.append(hblk)
        carry = hblk[SUBLANES - 1:SUBLANES, :]
    return jnp.concatenate(outs, axis=0), carry


def _even_kernel(h_ref, gain_ref, w_in_ref, conv_w_ref, conv_b_ref, wg_ref, ba_ref, bx_ref,
                 lam_ref, lb_logits_ref, hnorm_ref, w_out_ref, o_ref,
                 xa_ext, lru_carry, state_t, proj_b, mix, *, layer_j):
    t_blk = h_ref.shape[0]
    lw = conv_b_ref.shape[1]
    hw = hnorm_ref.shape[1]
    n_heads = hw // HGRN_HEAD_DIM

    @pl.when(pl.program_id(1) == 0)
    def _():
        xa_ext[0:SUBLANES, :] = jnp.zeros((SUBLANES, lw), F32)
        lru_carry[...] = jnp.zeros_like(lru_carry)
        state_t[...] = jnp.zeros_like(state_t)

    h = h_ref[...]
    xn = _rms_norm(h, gain_ref[...]).astype(BF16)

    pa = _dot(xn, w_in_ref[:, 0:2 * lw])
    xa = pa[:, 0:lw]
    ga = pa[:, lw:2 * lw]
    xa_ext[SUBLANES:, :] = xa
    ext = xa_ext[...]
    ua = conv_b_ref[...] + conv_w_ref[CONV_WIDTH - 1:CONV_WIDTH, :] * xa
    for k in range(CONV_WIDTH - 1):
        tap = pltpu.roll(ext, CONV_WIDTH - 1 - k, 0)[SUBLANES:, :]
        ua = ua + conv_w_ref[k:k + 1, :] * tap
    xa_ext[0:SUBLANES, :] = xa[t_blk - SUBLANES:, :]

    ua_b = ua.astype(BF16)
    gates = [_dot(ua_b[:, g * LRU_GROUP:(g + 1) * LRU_GROUP], wg_ref[g]) for g in range(lw // LRU_GROUP)]
    r = _sigmoid(jnp.concatenate([gt[:, :LRU_GROUP] for gt in gates], axis=1) + ba_ref[...])
    i_gate = _sigmoid(jnp.concatenate([gt[:, LRU_GROUP:] for gt in gates], axis=1) + bx_ref[...])
    neg_lam = -lam_ref[...]
    softplus = jnp.maximum(neg_lam, 0.0) + jnp.log1p(jnp.exp(-jnp.abs(neg_lam)))
    log_a = (-LRU_C * softplus) * r
    a = jnp.exp(log_a)
    th = jnp.tanh(log_a)
    beta = jnp.sqrt(jnp.maximum(-2.0 * th / (1.0 - th), 0.0))
    b_term = beta * (i_gate * ua)
    hl, carry = _lru_scan(a, b_term, lru_carry[0:1, :])
    lru_carry[0:1, :] = carry
    mix[:, 0:lw] = (hl * _gelu_tanh(ga)).astype(BF16)

    proj_b[...] = _dot(xn, w_in_ref[:, 2 * lw:])

    logits = lb_logits_ref[...]
    n_lb = logits.shape[0]
    mx = logits[0:1, :]
    for i in range(1, n_lb):
        mx = jnp.maximum(mx, logits[i:i + 1, :])
    es = [jnp.exp(logits[i:i + 1, :] - mx) for i in range(n_lb)]
    den = es[0]
    for e in es[1:]:
        den = den + e
    ps = [e / den for e in es]
    cum = ps[0]
    for p in ps[1:layer_j + 1]:
        cum = cum + p
    lb = cum - ps[0]
    one_m_lb = 1.0 - lb

    c = HGRN_CHUNK
    n_sub = c // HGRN_SUB
    ri = lax.broadcasted_iota(jnp.int32, (c, c), 0)
    ci = lax.broadcasted_iota(jnp.int32, (c, c), 1)
    tri = (ci <= ri).astype(BF16)
    diag_mask = (ci <= ri) & ((ri // HGRN_SUB) == (ci // HGRN_SUB))
    zeros_sub = jnp.zeros((HGRN_SUB, HGRN_HEAD_DIM), F32)

    def chunk_body(ck, _):
        r0 = pl.multiple_of(ck * c, c)
        rows = pl.ds(r0, c)
        z_all = proj_b[rows, hw:2 * hw]
        sg, sg_neg = _sigmoid_pair(z_all)
        k_all = one_m_lb * sg_neg
        log_f = jnp.log(lb + one_m_lb * sg)
        hi, mid, lo = _split3_bf16(log_f)
        bc = _dot(tri, jnp.concatenate([hi, mid, lo], axis=1))
        bcum_all = bc[:, 0:hw] + bc[:, hw:2 * hw] + bc[:, 2 * hw:3 * hw]
        for hd in range(n_heads):
            ln = slice(hd * HGRN_HEAD_DIM, (hd + 1) * HGRN_HEAD_DIM)
            qf = _silu(proj_b[rows, ln])
            kk = k_all[:, ln]
            vv = proj_b[rows, 2 * hw + hd * HGRN_HEAD_DIM:2 * hw + (hd + 1) * HGRN_HEAD_DIM].astype(BF16)
            gg = proj_b[rows, 3 * hw + hd * HGRN_HEAD_DIM:3 * hw + (hd + 1) * HGRN_HEAD_DIM]
            bcum = bcum_all[:, ln]
            m = [jnp.zeros((1, HGRN_HEAD_DIM), F32)] + [
                bcum[(i + 1) * HGRN_SUB - 1:(i + 1) * HGRN_SUB, :] for i in range(n_sub)]
            start = jnp.concatenate(
                [jnp.broadcast_to(m[i], (HGRN_SUB, HGRN_HEAD_DIM)) for i in range(n_sub)], axis=0)
            end = jnp.concatenate(
                [jnp.broadcast_to(m[i + 1], (HGRN_SUB, HGRN_HEAD_DIM)) for i in range(n_sub)], axis=0)
            q_d = qf * jnp.exp(bcum - start)
            k_d = kk * jnp.exp(start - bcum)
            k_e = kk * jnp.exp(end - bcum)
            q_sub = [q_d[i * HGRN_SUB:(i + 1) * HGRN_SUB, :] for i in range(n_sub)]
            k_sub = [k_e[i * HGRN_SUB:(i + 1) * HGRN_SUB, :] for i in range(n_sub)]
            q_inter = jnp.concatenate([q_sub[i] * jnp.exp(m[i]) for i in range(n_sub)], axis=0)
            k_dec = jnp.concatenate([k_sub[i] * jnp.exp(m[n_sub] - m[i + 1]) for i in range(n_sub)], axis=0)
            q_off = jnp.concatenate([
                jnp.concatenate([zeros_sub if i <= j else q_sub[i] * jnp.exp(m[i] - m[j + 1])
                                 for i in range(n_sub)], axis=0)
                for j in range(n_sub - 1)], axis=1)
            k_off = jnp.concatenate([
                jnp.concatenate([k_sub[i] if i == j else zeros_sub for i in range(n_sub)], axis=0)
                for j in range(n_sub - 1)], axis=1)
            attn = _dot_nt(q_off.astype(BF16), k_off.astype(BF16))
            attn = attn + jnp.where(diag_mask, _dot_nt(q_d.astype(BF16), k_d.astype(BF16)), 0.0)
            st = state_t[hd]
            o = _dot_nt(q_inter.astype(BF16), st.astype(BF16)) + _dot(attn.astype(BF16), vv)
            state_t[hd] = st * jnp.exp(m[n_sub]) + _dot_tn(vv, k_dec.astype(BF16))
            o = o * lax.rsqrt(jnp.mean(o * o, axis=-1, keepdims=True) + EPS)
            yb = o * hnorm_ref[:, ln] * _silu(gg)
            mix[rows, lw + hd * HGRN_HEAD_DIM:lw + (hd + 1) * HGRN_HEAD_DIM] = yb.astype(BF16)
        return 0

    lax.fori_loop(0, t_blk // c, chunk_body, 0)

    o_ref[...] = h + _dot(mix[...], w_out_ref[...])


def _even_call(h, gain, w_in, conv_w, conv_b, wg, ba, bx, lam, lb_logits, hnorm, w_out, layer_j):
    bsz, s, d = h.shape
    lw = conv_b.shape[1]
    hw = hnorm.shape[1]
    t = MIX_BLOCK_T
    return pl.pallas_call(
        functools.partial(_even_kernel, layer_j=layer_j),
        grid=(bsz, s // t),
        in_specs=[
            pl.BlockSpec((None, t, d), lambda b, i: (b, i, 0)),
            _const_spec(gain.shape),
            _const_spec(w_in.shape),
            _const_spec(conv_w.shape),
            _const_spec(conv_b.shape),
            _const_spec(wg.shape),
            _const_spec(ba.shape),
            _const_spec(bx.shape),
            _const_spec(lam.shape),
            _const_spec(lb_logits.shape),
            _const_spec(hnorm.shape),
            _const_spec(w_out.shape),
        ],
        out_specs=pl.BlockSpec((None, t, d), lambda b, i: (b, i, 0)),
        out_shape=jax.ShapeDtypeStruct((bsz, s, d), F32),
        scratch_shapes=[
            pltpu.VMEM((t + SUBLANES, lw), F32),
            pltpu.VMEM((SUBLANES, lw), F32),
            pltpu.VMEM((hw // HGRN_HEAD_DIM, HGRN_HEAD_DIM, HGRN_HEAD_DIM), F32),
            pltpu.VMEM((t, 4 * hw), F32),
            pltpu.VMEM((t, lw + hw), BF16),
        ],
        compiler_params=pltpu.CompilerParams(
            dimension_semantics=("parallel", "arbitrary"), vmem_limit_bytes=VMEM_LIMIT_BYTES),
        name="mix_even",
    )(h, gain, w_in, conv_w, conv_b, wg, ba, bx, lam, lb_logits, hnorm, w_out)


def _odd_kernel(h_ref, gain_ref, w_in_ref, sgu_w_ref, sgu_b_ref, pool_w_ref, pool_scale_ref,
                w_out_ref, o_ref, pd_ext, mix):
    t_blk = h_ref.shape[0]
    pw = pool_scale_ref.shape[1]
    sw = w_in_ref.shape[1] - pw
    sw //= 2
    gdim = sw // SGU_GROUPS
    pdim = pw // len(POOL_WINDOWS)

    @pl.when(pl.program_id(1) == 0)
    def _():
        pd_ext[0:POOL_HALO, :] = jnp.zeros((POOL_HALO, pw), F32)

    h = h_ref[...]
    xn = _rms_norm(h, gain_ref[...]).astype(BF16)

    zc = _gelu_tanh(_dot(xn, w_in_ref[:, 0:2 * sw]))
    u = zc[:, 0:sw]
    v = zc[:, sw:2 * sw]
    ri = lax.broadcasted_iota(jnp.int32, (SGU_CHUNK, SGU_CHUNK), 0)
    ci = lax.broadcasted_iota(jnp.int32, (SGU_CHUNK, SGU_CHUNK), 1)
    causal = ci <= ri
    for g in range(SGU_GROUPS):
        ln = slice(g * gdim, (g + 1) * gdim)
        vg = v[:, ln]
        mu = jnp.mean(vg, axis=-1, keepdims=True)
        cen = vg - mu
        var = jnp.mean(cen * cen, axis=-1, keepdims=True)
        vn = (cen * lax.rsqrt(var + EPS)).astype(BF16)
        wm = jnp.where(causal, sgu_w_ref[g], 0.0).astype(BF16)
        bias = sgu_b_ref[g]
        for n in range(t_blk // SGU_CHUNK):
            rows = slice(n * SGU_CHUNK, (n + 1) * SGU_CHUNK)
            sv = _dot(wm, vn[rows, :]) + bias
            mix[rows, ln] = (u[rows, ln] * sv).astype(BF16)

    pd_ext[POOL_HALO:, :] = _dot(xn, w_in_ref[:, 2 * sw:])
    pos = pl.program_id(1) * t_blk + lax.broadcasted_iota(jnp.int32, (t_blk, 1), 0) + 1
    for gi, win in enumerate(POOL_WINDOWS):
        ln = slice(gi * pdim, (gi + 1) * pdim)
        ext = pd_ext[:, ln]
        acc = ext
        sh = 1
        while sh < win:
            acc = acc + pltpu.roll(acc, sh, 0)
            sh *= 2
        inv_count = 1.0 / jnp.minimum(pos, win).astype(F32)
        pooled = acc[POOL_HALO:, :] * inv_count - ext[POOL_HALO:, :]
        yd = _dot(pooled.astype(BF16), pool_w_ref[gi]) * pool_scale_ref[:, ln]
        mix[:, sw + gi * pdim:sw + (gi + 1) * pdim] = yd.astype(BF16)
    pd_ext[0:POOL_HALO, :] = pd_ext[t_blk:t_blk + POOL_HALO, :]

    o_ref[...] = h + _dot(mix[...], w_out_ref[...])


def _odd_call(h, gain, w_in, sgu_w, sgu_b, pool_w, pool_scale, w_out):
    bsz, s, d = h.shape
    pw = pool_scale.shape[1]
    mw = w_out.shape[0]
    t = MIX_BLOCK_T
    return pl.pallas_call(
        _odd_kernel,
        grid=(bsz, s // t),
        in_specs=[
            pl.BlockSpec((None, t, d), lambda b, i: (b, i, 0)),
            _const_spec(gain.shape),
            _const_spec(w_in.shape),
            _const_spec(sgu_w.shape),
            _const_spec(sgu_b.shape),
            _const_spec(pool_w.shape),
            _const_spec(pool_scale.shape),
            _const_spec(w_out.shape),
        ],
        out_specs=pl.BlockSpec((None, t, d), lambda b, i: (b, i, 0)),
        out_shape=jax.ShapeDtypeStruct((bsz, s, d), F32),
        scratch_shapes=[
            pltpu.VMEM((t + POOL_HALO, pw), F32),
            pltpu.VMEM((t, mw), BF16),
        ],
        compiler_params=pltpu.CompilerParams(
            dimension_semantics=("parallel", "arbitrary"), vmem_limit_bytes=VMEM_LIMIT_BYTES),
        name="mix_odd",
    )(h, gain, w_in, sgu_w, sgu_b, pool_w, pool_scale, w_out)


def _gate_block_diag(wa, wx):
    n_heads, hd, _ = wa.shape
    per = LRU_GROUP // hd
    groups = n_heads // per
    out = jnp.zeros((groups, LRU_GROUP, 2 * LRU_GROUP), wa.dtype)
    for hh in range(n_heads):
        g, p = divmod(hh, per)
        out = out.at[g, p * hd:(p + 1) * hd, p * hd:(p + 1) * hd].set(wa[hh])
        out = out.at[g, p * hd:(p + 1) * hd, LRU_GROUP + p * hd:LRU_GROUP + (p + 1) * hd].set(wx[hh])
    return out


def kernel(x, norm_mix, norm_ffn, w_in_even, conv_w, conv_b, lru_wa, lru_ba, lru_wx, lru_bx, lru_lambda, hgrn_lb_logits, hgrn_norm, w_out_even, w_in_odd, sgu_w, sgu_b, pool_w, pool_scale, w_out_odd, w_ffn_in, w_ffn_out, norm_final):
    bsz, s, d = x.shape
    depth = norm_mix.shape[0]
    assert s % MIX_BLOCK_T == 0 and (bsz * s) % FFN_BLOCK_M == 0
    assert MIX_BLOCK_T % SGU_CHUNK == 0 and MIX_BLOCK_T % HGRN_CHUNK == 0
    row = lambda a: a.reshape(1, -1)
    h = x
    for layer in range(depth):
        j = layer // 2
        if layer % 2 == 0:
            wg = _gate_block_diag(lru_wa[j], lru_wx[j]).astype(BF16)
            h = _even_call(h, row(norm_mix[layer]), w_in_even[j].astype(BF16), conv_w[j], row(conv_b[j]),
                           wg, row(lru_ba[j]), row(lru_bx[j]), row(lru_lambda[j]), hgrn_lb_logits,
                           row(hgrn_norm[j]), w_out_even[j].astype(BF16), j)
        else:
            h = _odd_call(h, row(norm_mix[layer]), w_in_odd[j].astype(BF16), sgu_w[j],
                          sgu_b[j][..., None], pool_w[j].astype(BF16), row(pool_scale[j]),
                          w_out_odd[j].astype(BF16))
        h = _ffn_call(h.reshape(bsz * s, d), row(norm_ffn[layer]), w_ffn_in[layer].astype(BF16),
                      w_ffn_out[layer].astype(BF16), row(norm_final),
                      final_norm=(layer == depth - 1)).reshape(bsz, s, d)
    return h
```

```python
import functools

import jax
import jax.numpy as jnp
from jax import lax
from jax.experimental import pallas as pl
from jax.experimental.pallas import tpu as pltpu

F32 = jnp.float32
BF16 = jnp.bfloat16

EPS = 1e-6
LRU_C = 8.0
CONV_WIDTH = 4
LRU_HEADS = 8
LRU_GROUP = 256
HGRN_HEAD_DIM = 128
HGRN_CHUNK = 64
HGRN_SUB = 16
SGU_CHUNK = 128
SGU_GROUPS = 4
POOL_WINDOWS = (2, 4, 8, 16)
POOL_HALO = 16
SUBLANES = 8

MIX_BLOCK_T = 512
FFN_BLOCK_M = 512
FFN_CHUNK = 1024
MXU_PIECE = 256
VMEM_LIMIT_BYTES = 56 * 1024 * 1024


def _dot(a, b):
    return jnp.dot(a, b, preferred_element_type=F32)


def _dot_nt(a, b):
    return lax.dot_general(a, b, (((1,), (1,)), ((), ())), preferred_element_type=F32)


def _dot_tn(a, b):
    return lax.dot_general(a, b, (((0,), (0,)), ((), ())), preferred_element_type=F32)


def _rms_norm(x, gain):
    ms = jnp.mean(x * x, axis=-1, keepdims=True)
    return x * lax.rsqrt(ms + EPS) * gain


def _sigmoid(x):
    return 0.5 * jnp.tanh(0.5 * x) + 0.5


def _sigmoid_small_tail(x):
    e = jnp.exp(-jnp.abs(x))
    big = 1.0 / (1.0 + e)
    return jnp.where(x >= 0, big, e * big)


def _silu(x):
    hx = 0.5 * x
    return hx * jnp.tanh(hx) + hx


def _gelu_tanh(x):
    c = 0.7978845608028654
    return 0.5 * x * (1.0 + jnp.tanh(c * (x + 0.044715 * (x * x * x))))


def _split2_bf16(x):
    hi = x.astype(BF16)
    lo = (x - hi.astype(F32)).astype(BF16)
    return hi, lo


def _ffn_kernel(h_ref, gain_ref, w1_ref, w2_ref, gfin_ref, o_ref, *, final_norm):
    h = h_ref[...]
    hn = _rms_norm(h, gain_ref[...]).astype(BF16)
    acc = h
    d_ff = w1_ref.shape[1]
    for c in range(d_ff // FFN_CHUNK):
        cols = slice(c * FFN_CHUNK, (c + 1) * FFN_CHUNK)
        hid = _dot(hn, w1_ref[:, cols])
        hid = jnp.square(jnp.maximum(hid, 0.0)).astype(BF16)
        acc = acc + _dot(hid, w2_ref[cols, :])
    if final_norm:
        acc = _rms_norm(acc, gfin_ref[...])
    o_ref[...] = acc


def _const_spec(shape):
    nd = len(shape)
    return pl.BlockSpec(shape, lambda *_: (0,) * nd, pipeline_mode=pl.Buffered(1))


def _ffn_call(h2d, gain, w1, w2, gfin, final_norm):
    m, d = h2d.shape
    d_ff = w1.shape[1]
    return pl.pallas_call(
        functools.partial(_ffn_kernel, final_norm=final_norm),
        grid=(m // FFN_BLOCK_M,),
        in_specs=[
            pl.BlockSpec((FFN_BLOCK_M, d), lambda i: (i, 0)),
            _const_spec((1, d)),
            _const_spec((d, d_ff)),
            _const_spec((d_ff, d)),
            _const_spec((1, d)),
        ],
        out_specs=pl.BlockSpec((FFN_BLOCK_M, d), lambda i: (i, 0)),
        out_shape=jax.ShapeDtypeStruct((m, d), F32),
        compiler_params=pltpu.CompilerParams(
            dimension_semantics=("parallel",), vmem_limit_bytes=VMEM_LIMIT_BYTES),
        name="ffn_final" if final_norm else "ffn",
    )(h2d, gain, w1, w2, gfin)


def _lru_scan(a, b, carry):
    t, w = a.shape
    g = t // SUBLANES
    a3 = a.reshape(g, SUBLANES, w)
    b3 = b.reshape(g, SUBLANES, w)
    row = lax.broadcasted_iota(jnp.int32, (g, SUBLANES, w), 1)
    sh = 1
    while sh < SUBLANES:
        keep = row >= sh
        a_prev = jnp.where(keep, pltpu.roll(a3, sh, 1), 1.0)
        b_prev = jnp.where(keep, pltpu.roll(b3, sh, 1), 0.0)
        b3 = a3 * b_prev + b3
        a3 = a3 * a_prev
        sh *= 2

    def chain():
        c = carry
        outs = []
        for i in range(g):
            hblk = a3[i] * c + b3[i]
            outs.append(hblk)
            c = hblk[SUBLANES - 1:SUBLANES, :]
        return jnp.concatenate(outs, axis=0), c

    return chain


def _even_kernel(h_ref, gain_ref, w_in_ref, conv_w_ref, conv_b_ref, wg_ref, ba_ref, bx_ref,
                 lam_ref, lb_logits_ref, hnorm_ref, w_out_ref, o_ref,
                 proj, hres, lru_carry, state_t, mix, *, layer_j, n_t):
    t_blk = h_ref.shape[0]
    d_model = h_ref.shape[1]
    lw = conv_b_ref.shape[1]
    hw = hnorm_ref.shape[1]
    n_heads = hw // HGRN_HEAD_DIM
    hdr = SUBLANES
    g = pl.program_id(0)
    slot_w = g % 2
    slot_r = 1 - slot_w
    blk_r = jnp.maximum(g - 1, 0)

    @pl.when(g == 0)
    def _():
        proj[1] = jnp.zeros(proj.shape[1:], F32)
        hres[1] = jnp.zeros(hres.shape[1:], F32)

    @pl.when(blk_r % n_t == 0)
    def _():
        lru_carry[...] = jnp.zeros_like(lru_carry)
        state_t[...] = jnp.zeros_like(state_t)

    h_new = h_ref[...]
    hres[slot_w] = h_new
    xn = _rms_norm(h_new, gain_ref[...]).astype(BF16)
    fillers = []

    def add_proj_piece(c0):
        def piece():
            proj[slot_w, hdr:, c0:c0 + MXU_PIECE] = _dot(xn, w_in_ref[:, c0:c0 + MXU_PIECE])
        fillers.append(piece)

    for c0 in range(0, w_in_ref.shape[1], MXU_PIECE):
        add_proj_piece(c0)

    def fill():
        if fillers:
            fillers.pop(0)()

    pr = proj.at[slot_r]

    neg_lam = -lam_ref[...]
    softplus = jnp.maximum(neg_lam, 0.0) + jnp.log1p(jnp.exp(-jnp.abs(neg_lam)))
    log_a_scale = -LRU_C * softplus
    for gi in range(lw // LRU_GROUP):
        ln = slice(gi * LRU_GROUP, (gi + 1) * LRU_GROUP)
        xa = pr[hdr:, ln]
        ua = conv_b_ref[:, ln] + conv_w_ref[CONV_WIDTH - 1:CONV_WIDTH, ln] * xa
        for k in range(CONV_WIDTH - 1):
            r0 = hdr - (CONV_WIDTH - 1) + k
            ua = ua + conv_w_ref[k:k + 1, ln] * pr[r0:r0 + t_blk, ln]
        tail = pr[t_blk:t_blk + hdr, ln]
        proj[slot_w, 0:hdr, ln] = jnp.where(g % n_t == 0, 0.0, tail)
        gates = _dot(ua.astype(BF16), wg_ref[gi])
        fill()
        r = _sigmoid(gates[:, :LRU_GROUP] + ba_ref[:, ln])
        i_gate = _sigmoid(gates[:, LRU_GROUP:] + bx_ref[:, ln])
        log_a = log_a_scale[:, ln] * r
        a = jnp.exp(log_a)
        th = jnp.tanh(log_a)
        beta = jnp.sqrt(jnp.maximum(-2.0 * th / (1.0 - th), 0.0))
        b_term = beta * (i_gate * ua)
        fill()
        chain = _lru_scan(a, b_term, lru_carry[0:1, ln])
        fill()
        hl, carry = chain()
        lru_carry[0:1, ln] = carry
        fill()
        mix[:, ln] = (hl * _gelu_tanh(pr[hdr:, lw + gi * LRU_GROUP:lw + (gi + 1) * LRU_GROUP])).astype(BF16)
        fill()

    while fillers:
        fill()

    logits = lb_logits_ref[...]
    n_lb = logits.shape[0]
    mx = logits[0:1, :]
    for i in range(1, n_lb):
        mx = jnp.maximum(mx, logits[i:i + 1, :])
    es = [jnp.exp(logits[i:i + 1, :] - mx) for i in range(n_lb)]
    den = es[0]
    for e in es[1:]:
        den = den + e
    ps = [e / den for e in es]
    cum = ps[0]
    for p in ps[1:layer_j + 1]:
        cum = cum + p
    lb = cum - ps[0]
    one_m_lb = 1.0 - lb

    c = HGRN_CHUNK
    d = HGRN_HEAD_DIM
    n_sub = c // HGRN_SUB
    n_chunks = t_blk // c
    ri = lax.broadcasted_iota(jnp.int32, (c, c), 0)
    ci = lax.broadcasted_iota(jnp.int32, (c, c), 1)
    tri = (ci <= ri).astype(BF16)
    diag_mask = (ci <= ri) & ((ri // HGRN_SUB) == (ci // HGRN_SUB))
    zeros_sub = jnp.zeros((HGRN_SUB, d), BF16)
    base = 2 * lw

    def stage_gates(ck):
        r0 = hdr + ck * c
        f_gate = lb + one_m_lb * _sigmoid_small_tail(pr[r0:r0 + c, base + hw:base + 2 * hw])
        hi, lo = _split2_bf16(jnp.log(f_gate))
        bc = _dot(tri, jnp.concatenate([hi, lo], axis=1))
        return 1.0 - f_gate, bc

    def stage_scores(ck, hd, k_all, bc):
        r0 = hdr + ck * c
        ln = slice(hd * d, (hd + 1) * d)
        c_q = base + hd * d
        bcum = bc[:, ln] + bc[:, hw + hd * d:hw + (hd + 1) * d]
        kk = k_all[:, ln]
        qf = _silu(pr[r0:r0 + c, c_q:c_q + d])
        vv = pr[r0:r0 + c, c_q + 2 * hw:c_q + 2 * hw + d].astype(BF16)
        m = [jnp.zeros((1, d), F32)] + [bcum[(i + 1) * HGRN_SUB - 1:(i + 1) * HGRN_SUB, :] for i in range(n_sub)]
        start = jnp.concatenate([jnp.broadcast_to(m[i], (HGRN_SUB, d)) for i in range(n_sub)], axis=0)
        dec = jnp.exp(bcum - start)
        q_d = qf * dec
        k_d = kk * (1.0 / dec)
        q_sub = [q_d[i * HGRN_SUB:(i + 1) * HGRN_SUB, :] for i in range(n_sub)]
        k_sub = [k_d[i * HGRN_SUB:(i + 1) * HGRN_SUB, :] * jnp.exp(m[i + 1] - m[i]) for i in range(n_sub)]
        q_inter = jnp.concatenate([(q_sub[i] * jnp.exp(m[i])).astype(BF16) for i in range(n_sub)], axis=0)
        k_dec = jnp.concatenate(
            [(k_sub[i] * jnp.exp(m[n_sub] - m[i + 1])).astype(BF16) for i in range(n_sub)], axis=0)
        q_off = jnp.concatenate([
            jnp.concatenate([zeros_sub if i <= j else (q_sub[i] * jnp.exp(m[i] - m[j + 1])).astype(BF16)
                             for i in range(n_sub)], axis=0)
            for j in range(n_sub - 1)], axis=1)
        k_off = jnp.concatenate([
            jnp.concatenate([k_sub[i].astype(BF16) if i == j else zeros_sub for i in range(n_sub)], axis=0)
            for j in range(n_sub - 1)], axis=1)
        a_off = _dot_nt(q_off, k_off)
        a_diag = _dot_nt(q_d.astype(BF16), k_d.astype(BF16))
        return a_off, a_diag, k_dec, q_inter, vv, jnp.exp(m[n_sub])

    def stage_output(sc, st):
        a_off, a_diag, k_dec, q_inter, vv, e_last = sc
        attn = a_off + jnp.where(diag_mask, a_diag, 0.0)
        o = _dot_nt(q_inter, st.astype(BF16)) + _dot(attn.astype(BF16), vv)
        return o, st * e_last + _dot_tn(vv, k_dec)

    def stage_store(ck, hd, o):
        r0 = hdr + ck * c
        c_g = base + 3 * hw + hd * d
        o = o * lax.rsqrt(jnp.mean(o * o, axis=-1, keepdims=True) + EPS)
        yb = o * hnorm_ref[:, hd * d:(hd + 1) * d] * _silu(pr[r0:r0 + c, c_g:c_g + d])
        mix[ck * c:(ck + 1) * c, lw + hd * d:lw + (hd + 1) * d] = yb.astype(BF16)

    states = [state_t[hd] for hd in range(n_heads)]
    for ck in range(n_chunks):
        k_all, bc = stage_gates(ck)
        for hd in range(n_heads):
            o, states[hd] = stage_output(stage_scores(ck, hd, k_all, bc), states[hd])
            stage_store(ck, hd, o)
    for hd in range(n_heads):
        state_t[hd] = states[hd]

    o_ref[...] = hres[slot_r] + _dot(mix[...], w_out_ref[...])


def _even_call(h, gain, w_in, conv_w, conv_b, wg, ba, bx, lam, lb_logits, hnorm, w_out, layer_j):
    bsz, s, d = h.shape
    lw = conv_b.shape[1]
    hw = hnorm.shape[1]
    t = MIX_BLOCK_T
    n_t = s // t
    n_blk = bsz * n_t

    def in_map(g):
        blk = jnp.minimum(g, n_blk - 1)
        return (blk // n_t, blk % n_t, 0)

    def out_map(g):
        blk = jnp.maximum(g - 1, 0)
        return (blk // n_t, blk % n_t, 0)

    return pl.pallas_call(
        functools.partial(_even_kernel, layer_j=layer_j, n_t=n_t),
        grid=(n_blk + 1,),
        in_specs=[
            pl.BlockSpec((None, t, d), in_map),
            _const_spec(gain.shape),
            _const_spec(w_in.shape),
            _const_spec(conv_w.shape),
            _const_spec(conv_b.shape),
            _const_spec(wg.shape),
            _const_spec(ba.shape),
            _const_spec(bx.shape),
            _const_spec(lam.shape),
            _const_spec(lb_logits.shape),
            _const_spec(hnorm.shape),
            _const_spec(w_out.shape),
        ],
        out_specs=pl.BlockSpec((None, t, d), out_map),
        out_shape=jax.ShapeDtypeStruct((bsz, s, d), F32),
        scratch_shapes=[
            pltpu.VMEM((2, t + SUBLANES, w_in.shape[1]), F32),
            pltpu.VMEM((2, t, d), F32),
            pltpu.VMEM((SUBLANES, lw), F32),
            pltpu.VMEM((hw // HGRN_HEAD_DIM, HGRN_HEAD_DIM, HGRN_HEAD_DIM), F32),
            pltpu.VMEM((t, lw + hw), BF16),
        ],
        compiler_params=pltpu.CompilerParams(
            dimension_semantics=("arbitrary",), vmem_limit_bytes=VMEM_LIMIT_BYTES),
        name="mix_even",
    )(h, gain, w_in, conv_w, conv_b, wg, ba, bx, lam, lb_logits, hnorm, w_out)


def _odd_kernel(h_ref, gain_ref, w_in_ref, sgu_w_ref, sgu_b_ref, pool_w_ref, pool_scale_ref,
                w_out_ref, o_ref, pd_ext, mix):
    t_blk = h_ref.shape[0]
    pw = pool_scale_ref.shape[1]
    sw = w_in_ref.shape[1] - pw
    sw //= 2
    gdim = sw // SGU_GROUPS
    pdim = pw // len(POOL_WINDOWS)

    @pl.when(pl.program_id(1) == 0)
    def _():
        pd_ext[0:POOL_HALO, :] = jnp.zeros((POOL_HALO, pw), F32)

    h = h_ref[...]
    xn = _rms_norm(h, gain_ref[...]).astype(BF16)

    zc = _gelu_tanh(_dot(xn, w_in_ref[:, 0:2 * sw]))
    u = zc[:, 0:sw]
    v = zc[:, sw:2 * sw]
    ri = lax.broadcasted_iota(jnp.int32, (SGU_CHUNK, SGU_CHUNK), 0)
    ci = lax.broadcasted_iota(jnp.int32, (SGU_CHUNK, SGU_CHUNK), 1)
    causal = ci <= ri
    for g in range(SGU_GROUPS):
        ln = slice(g * gdim, (g + 1) * gdim)
        vg = v[:, ln]
        mu = jnp.mean(vg, axis=-1, keepdims=True)
        cen = vg - mu
        var = jnp.mean(cen * cen, axis=-1, keepdims=True)
        vn = (cen * lax.rsqrt(var + EPS)).astype(BF16)
        wm = jnp.where(causal, sgu_w_ref[g], 0.0).astype(BF16)
        bias = sgu_b_ref[g]
        for n in range(t_blk // SGU_CHUNK):
            rows = slice(n * SGU_CHUNK, (n + 1) * SGU_CHUNK)
            sv = _dot(wm, vn[rows, :]) + bias
            mix[rows, ln] = (u[rows, ln] * sv).astype(BF16)

    pd_ext[POOL_HALO:, :] = _dot(xn, w_in_ref[:, 2 * sw:])
    pos = pl.program_id(1) * t_blk + lax.broadcasted_iota(jnp.int32, (t_blk, 1), 0) + 1
    for gi, win in enumerate(POOL_WINDOWS):
        ln = slice(gi * pdim, (gi + 1) * pdim)
        ext = pd_ext[:, ln]
        acc = ext
        sh = 1
        while sh < win:
            acc = acc + pltpu.roll(acc, sh, 0)
            sh *= 2
        inv_count = 1.0 / jnp.minimum(pos, win).astype(F32)
        pooled = acc[POOL_HALO:, :] * inv_count - ext[POOL_HALO:, :]
        yd = _dot(pooled.astype(BF16), pool_w_ref[gi]) * pool_scale_ref[:, ln]
        mix[:, sw + gi * pdim:sw + (gi + 1) * pdim] = yd.astype(BF16)
    pd_ext[0:POOL_HALO, :] = pd_ext[t_blk:t_blk + POOL_HALO, :]

    o_ref[...] = h + _dot(mix[...], w_out_ref[...])


def _odd_call(h, gain, w_in, sgu_w, sgu_b, pool_w, pool_scale, w_out):
    bsz, s, d = h.shape
    pw = pool_scale.shape[1]
    mw = w_out.shape[0]
    t = MIX_BLOCK_T
    return pl.pallas_call(
        _odd_kernel,
        grid=(bsz, s // t),
        in_specs=[
            pl.BlockSpec((None, t, d), lambda b, i: (b, i, 0)),
            _const_spec(gain.shape),
            _const_spec(w_in.shape),
            _const_spec(sgu_w.shape),
            _const_spec(sgu_b.shape),
            _const_spec(pool_w.shape),
            _const_spec(pool_scale.shape),
            _const_spec(w_out.shape),
        ],
        out_specs=pl.BlockSpec((None, t, d), lambda b, i: (b, i, 0)),
        out_shape=jax.ShapeDtypeStruct((bsz, s, d), F32),
        scratch_shapes=[
            pltpu.VMEM((t + POOL_HALO, pw), F32),
            pltpu.VMEM((t, mw), BF16),
        ],
        compiler_params=pltpu.CompilerParams(
            dimension_semantics=("parallel", "arbitrary"), vmem_limit_bytes=VMEM_LIMIT_BYTES),
        name="mix_odd",
    )(h, gain, w_in, sgu_w, sgu_b, pool_w, pool_scale, w_out)


def _gate_block_diag(wa, wx):
    n_heads, hd, _ = wa.shape
    per = LRU_GROUP // hd
    groups = n_heads // per
    out = jnp.zeros((groups, LRU_GROUP, 2 * LRU_GROUP), wa.dtype)
    for hh in range(n_heads):
        g, p = divmod(hh, per)
        out = out.at[g, p * hd:(p + 1) * hd, p * hd:(p + 1) * hd].set(wa[hh])
        out = out.at[g, p * hd:(p + 1) * hd, LRU_GROUP + p * hd:LRU_GROUP + (p + 1) * hd].set(wx[hh])
    return out


def kernel(x, norm_mix, norm_ffn, w_in_even, conv_w, conv_b, lru_wa, lru_ba, lru_wx, lru_bx, lru_lambda, hgrn_lb_logits, hgrn_norm, w_out_even, w_in_odd, sgu_w, sgu_b, pool_w, pool_scale, w_out_odd, w_ffn_in, w_ffn_out, norm_final):
    bsz, s, d = x.shape
    depth = norm_mix.shape[0]
    assert s % MIX_BLOCK_T == 0 and (bsz * s) % FFN_BLOCK_M == 0
    assert MIX_BLOCK_T % SGU_CHUNK == 0 and MIX_BLOCK_T % HGRN_CHUNK == 0
    row = lambda a: a.reshape(1, -1)
    h = x
    for layer in range(depth):
        j = layer // 2
        if layer % 2 == 0:
            wg = _gate_block_diag(lru_wa[j], lru_wx[j]).astype(BF16)
            h = _even_call(h, row(norm_mix[layer]), w_in_even[j].astype(BF16), conv_w[j], row(conv_b[j]),
                           wg, row(lru_ba[j]), row(lru_bx[j]), row(lru_lambda[j]), hgrn_lb_logits,
                           row(hgrn_norm[j]), w_out_even[j].astype(BF16), j)
        else:
            h = _odd_call(h, row(norm_mix[layer]), w_in_odd[j].astype(BF16), sgu_w[j],
                          sgu_b[j][..., None], pool_w[j].astype(BF16), row(pool_scale[j]),
                          w_out_odd[j].astype(BF16))
        h = _ffn_call(h.reshape(bsz * s, d), row(norm_ffn[layer]), w_ffn_in[layer].astype(BF16),
                      w_ffn_out[layer].astype(BF16), row(norm_final),
                      final_norm=(layer == depth - 1)).reshape(bsz, s, d)
    return h
```

```python
import functools

import jax
import jax.numpy as jnp
from jax import lax
from jax.experimental import pallas as pl
from jax.experimental.pallas import tpu as pltpu

F32 = jnp.float32
BF16 = jnp.bfloat16

EPS = 1e-6
LRU_C = 8.0
CONV_WIDTH = 4
LRU_HEADS = 8
LRU_GROUP = 256
HGRN_HEAD_DIM = 128
HGRN_CHUNK = 64
HGRN_SUB = 16
SGU_CHUNK = 128
SGU_GROUPS = 4
POOL_WINDOWS = (2, 4, 8, 16)
POOL_HALO = 16
SUBLANES = 8

MIX_BLOCK_T = 512
FFN_BLOCK_M = 512
FFN_CHUNK = 1024
MXU_PIECE = 256
VMEM_LIMIT_BYTES = 56 * 1024 * 1024


def _dot(a, b):
    return jnp.dot(a, b, preferred_element_type=F32)


def _dot_nt(a, b):
    return lax.dot_general(a, b, (((1,), (1,)), ((), ())), preferred_element_type=F32)


def _dot_tn(a, b):
    return lax.dot_general(a, b, (((0,), (0,)), ((), ())), preferred_element_type=F32)


def _rms_norm(x, gain):
    ms = jnp.mean(x * x, axis=-1, keepdims=True)
    return x * lax.rsqrt(ms + EPS) * gain


def _sigmoid(x):
    return 0.5 * jnp.tanh(0.5 * x) + 0.5


def _sigmoid_small_tail(x):
    e = jnp.exp(-jnp.abs(x))
    big = 1.0 / (1.0 + e)
    return jnp.where(x >= 0, big, e * big)


def _silu(x):
    hx = 0.5 * x
    return hx * jnp.tanh(hx) + hx


def _gelu_tanh(x):
    c = 0.7978845608028654
    return 0.5 * x * (1.0 + jnp.tanh(c * (x + 0.044715 * (x * x * x))))


def _split2_bf16(x):
    hi = x.astype(BF16)
    lo = (x - hi.astype(F32)).astype(BF16)
    return hi, lo


def _const_spec(shape):
    nd = len(shape)
    return pl.BlockSpec(shape, lambda *_: (0,) * nd, pipeline_mode=pl.Buffered(1))


def _ffn_kernel(h_ref, gain_ref, w1_ref, w2_ref, gfin_ref, o_ref, *, final_norm):
    h = h_ref[...]
    hn = _rms_norm(h, gain_ref[...]).astype(BF16)
    acc = h
    d_ff = w1_ref.shape[1]
    for c in range(d_ff // FFN_CHUNK):
        cols = slice(c * FFN_CHUNK, (c + 1) * FFN_CHUNK)
        hid = _dot(hn, w1_ref[:, cols])
        hid = jnp.square(jnp.maximum(hid, 0.0)).astype(BF16)
        acc = acc + _dot(hid, w2_ref[cols, :])
    if final_norm:
        acc = _rms_norm(acc, gfin_ref[...])
    o_ref[...] = acc


def _ffn_call(h2d, gain, w1, w2, gfin, final_norm):
    m, d = h2d.shape
    d_ff = w1.shape[1]
    return pl.pallas_call(
        functools.partial(_ffn_kernel, final_norm=final_norm),
        grid=(m // FFN_BLOCK_M,),
        in_specs=[
            pl.BlockSpec((FFN_BLOCK_M, d), lambda i: (i, 0)),
            _const_spec((1, d)),
            _const_spec((d, d_ff)),
            _const_spec((d_ff, d)),
            _const_spec((1, d)),
        ],
        out_specs=pl.BlockSpec((FFN_BLOCK_M, d), lambda i: (i, 0)),
        out_shape=jax.ShapeDtypeStruct((m, d), F32),
        compiler_params=pltpu.CompilerParams(
            dimension_semantics=("parallel",), vmem_limit_bytes=VMEM_LIMIT_BYTES),
        name="ffn_final" if final_norm else "ffn",
    )(h2d, gain, w1, w2, gfin)


def _pipeline_maps(n_blk, n_t):
    def in_map(g):
        blk = jnp.minimum(g, n_blk - 1)
        return (blk // n_t, blk % n_t, 0)

    def out_map(g):
        blk = jnp.maximum(g - 1, 0)
        return (blk // n_t, blk % n_t, 0)

    return in_map, out_map


def _projection_pieces(xn, w_in_ref, pnew):
    def make(c0):
        def piece():
            pnew[:, c0:c0 + MXU_PIECE] = _dot(xn, w_in_ref[:, c0:c0 + MXU_PIECE])
        return piece

    return [make(c0) for c0 in range(0, w_in_ref.shape[1], MXU_PIECE)]


def _lru_scan(a, b, carry):
    t, w = a.shape
    g = t // SUBLANES
    a3 = a.reshape(g, SUBLANES, w)
    b3 = b.reshape(g, SUBLANES, w)
    row = lax.broadcasted_iota(jnp.int32, (g, SUBLANES, w), 1)
    sh = 1
    while sh < SUBLANES:
        keep = row >= sh
        a_prev = jnp.where(keep, pltpu.roll(a3, sh, 1), 1.0)
        b_prev = jnp.where(keep, pltpu.roll(b3, sh, 1), 0.0)
        b3 = a3 * b_prev + b3
        a3 = a3 * a_prev
        sh *= 2

    def chain():
        c = carry
        outs = []
        for i in range(g):
            hblk = a3[i] * c + b3[i]
            outs.append(hblk)
            c = hblk[SUBLANES - 1:SUBLANES, :]
        return jnp.concatenate(outs, axis=0), c

    return chain


def _block_diag2(a, b):
    za = jnp.zeros(a.shape, a.dtype)
    zb = jnp.zeros(b.shape, b.dtype)
    return jnp.concatenate([jnp.concatenate([a, zb], axis=1), jnp.concatenate([za, b], axis=1)], axis=0)


def _even_kernel(h_ref, hprev_ref, gain_ref, w_in_ref, conv_w_ref, conv_b_ref, wg_ref, ba_ref, bx_ref,
                 lam_ref, lb_logits_ref, hnorm_ref, w_out_ref, o_ref,
                 pcur, pnew, lru_carry, state_t, mix, *, layer_j, n_t):
    t_blk = h_ref.shape[0]
    lw = conv_b_ref.shape[1]
    hw = hnorm_ref.shape[1]
    n_heads = hw // HGRN_HEAD_DIM
    hdr = SUBLANES
    g = pl.program_id(0)
    blk_r = jnp.maximum(g - 1, 0)

    @pl.when(g == 0)
    def _():
        pcur[...] = jnp.zeros(pcur.shape, F32)

    @pl.when(blk_r % n_t == 0)
    def _():
        lru_carry[...] = jnp.zeros_like(lru_carry)
        state_t[...] = jnp.zeros_like(state_t)

    xn = _rms_norm(h_ref[...], gain_ref[...]).astype(BF16)
    fillers = _projection_pieces(xn, w_in_ref, pnew)

    def fill():
        if fillers:
            fillers.pop(0)()

    pr = pcur
    fill()
    fill()
    fill()

    neg_lam = -lam_ref[...]
    softplus = jnp.maximum(neg_lam, 0.0) + jnp.log1p(jnp.exp(-jnp.abs(neg_lam)))
    log_a_scale = -LRU_C * softplus
    for gi in range(lw // LRU_GROUP):
        ln = slice(gi * LRU_GROUP, (gi + 1) * LRU_GROUP)
        ua = conv_b_ref[:, ln] + conv_w_ref[CONV_WIDTH - 1:CONV_WIDTH, ln] * pr[hdr:, ln]
        for k in range(CONV_WIDTH - 1):
            r0 = hdr - (CONV_WIDTH - 1) + k
            ua = ua + conv_w_ref[k:k + 1, ln] * pr[r0:r0 + t_blk, ln]
        gates = _dot(ua.astype(BF16), wg_ref[gi])
        fill()
        r = _sigmoid(gates[:, :LRU_GROUP] + ba_ref[:, ln])
        i_gate = _sigmoid(gates[:, LRU_GROUP:] + bx_ref[:, ln])
        log_a = log_a_scale[:, ln] * r
        a = jnp.exp(log_a)
        th = jnp.tanh(log_a)
        beta = jnp.sqrt(jnp.maximum(-2.0 * th / (1.0 - th), 0.0))
        b_term = beta * (i_gate * ua)
        chain = _lru_scan(a, b_term, lru_carry[0:1, ln])
        fill()
        hl, carry = chain()
        lru_carry[0:1, ln] = carry
        fill()
        mix[:, ln] = (hl * _gelu_tanh(pr[hdr:, lw + gi * LRU_GROUP:lw + (gi + 1) * LRU_GROUP])).astype(BF16)
        if gi == 0:
            fill()

    while fillers:
        fill()

    logits = lb_logits_ref[...]
    n_lb = logits.shape[0]
    mx = logits[0:1, :]
    for i in range(1, n_lb):
        mx = jnp.maximum(mx, logits[i:i + 1, :])
    es = [jnp.exp(logits[i:i + 1, :] - mx) for i in range(n_lb)]
    den = es[0]
    for e in es[1:]:
        den = den + e
    ps = [e / den for e in es]
    cum = ps[0]
    for p in ps[1:layer_j + 1]:
        cum = cum + p
    lb = cum - ps[0]
    one_m_lb = 1.0 - lb

    c = HGRN_CHUNK
    d = HGRN_HEAD_DIM
    n_sub = c // HGRN_SUB
    n_chunks = t_blk // c
    ri = lax.broadcasted_iota(jnp.int32, (c, c), 0)
    ci = lax.broadcasted_iota(jnp.int32, (c, c), 1)
    tri = (ci <= ri).astype(BF16)
    rp = lax.broadcasted_iota(jnp.int32, (c, 2 * c), 0)
    cp = lax.broadcasted_iota(jnp.int32, (c, 2 * c), 1) % c
    diag_mask = (cp <= rp) & ((rp // HGRN_SUB) == (cp // HGRN_SUB))
    zeros_sub = jnp.zeros((HGRN_SUB, d), BF16)
    base = 2 * lw

    def stage_gates(ck):
        r0 = hdr + ck * c
        f_gate = lb + one_m_lb * _sigmoid_small_tail(pr[r0:r0 + c, base + hw:base + 2 * hw])
        hi, lo = _split2_bf16(jnp.log(f_gate))
        bc = _dot(tri, jnp.concatenate([hi, lo], axis=1))
        return 1.0 - f_gate, bc

    def stage_scores(ck, hd, k_all, bc):
        r0 = hdr + ck * c
        ln = slice(hd * d, (hd + 1) * d)
        c_q = base + hd * d
        bcum = bc[:, ln] + bc[:, hw + hd * d:hw + (hd + 1) * d]
        kk = k_all[:, ln]
        qf = _silu(pr[r0:r0 + c, c_q:c_q + d])
        vv = pr[r0:r0 + c, c_q + 2 * hw:c_q + 2 * hw + d].astype(BF16)
        m = [jnp.zeros((1, d), F32)] + [bcum[(i + 1) * HGRN_SUB - 1:(i + 1) * HGRN_SUB, :] for i in range(n_sub)]
        start = jnp.concatenate([jnp.broadcast_to(m[i], (HGRN_SUB, d)) for i in range(n_sub)], axis=0)
        dec = jnp.exp(bcum - start)
        q_d = qf * dec
        k_d = kk * (1.0 / dec)
        q_sub = [q_d[i * HGRN_SUB:(i + 1) * HGRN_SUB, :] for i in range(n_sub)]
        k_sub = [k_d[i * HGRN_SUB:(i + 1) * HGRN_SUB, :] * jnp.exp(m[i + 1] - m[i]) for i in range(n_sub)]
        q_inter = jnp.concatenate([(q_sub[i] * jnp.exp(m[i])).astype(BF16) for i in range(n_sub)], axis=0)
        k_dec = jnp.concatenate(
            [(k_sub[i] * jnp.exp(m[n_sub] - m[i + 1])).astype(BF16) for i in range(n_sub)], axis=0)
        q_off = jnp.concatenate([
            jnp.concatenate([zeros_sub if i <= j else (q_sub[i] * jnp.exp(m[i] - m[j + 1])).astype(BF16)
                             for i in range(n_sub)], axis=0)
            for j in range(n_sub - 1)], axis=1)
        k_off = jnp.concatenate([
            jnp.concatenate([k_sub[i].astype(BF16) if i == j else zeros_sub for i in range(n_sub)], axis=0)
            for j in range(n_sub - 1)], axis=1)
        return q_off, k_off, q_d.astype(BF16), k_d.astype(BF16), k_dec, q_inter, vv, jnp.exp(m[n_sub])

    def stage_attn(sa, sb):
        q_off_a, k_off_a, q_d_a, k_d_a, k_dec_a, q_inter_a, vv_a, e_last_a = sa
        q_off_b, k_off_b, q_d_b, k_d_b, k_dec_b, q_inter_b, vv_b, e_last_b = sb
        a_off = _dot_nt(jnp.concatenate([q_off_a, q_off_b], axis=1), _block_diag2(k_off_a, k_off_b))
        a_diag = _dot_nt(jnp.concatenate([q_d_a, q_d_b], axis=1), _block_diag2(k_d_a, k_d_b))
        upd_a = _dot_tn(vv_a, k_dec_a)
        upd_b = _dot_tn(vv_b, k_dec_b)
        q_inter = jnp.concatenate([q_inter_a, q_inter_b], axis=1)
        return a_off, a_diag, upd_a, upd_b, q_inter, _block_diag2(vv_a, vv_b), e_last_a, e_last_b

    def stage_output(at, st_a, st_b):
        a_off, a_diag, upd_a, upd_b, q_inter, vv2, e_last_a, e_last_b = at
        attn = a_off + jnp.where(diag_mask, a_diag, 0.0)
        o = _dot_nt(q_inter, _block_diag2(st_a.astype(BF16), st_b.astype(BF16)))
        o = o + _dot(attn.astype(BF16), vv2)
        return o[:, 0:d], o[:, d:2 * d], st_a * e_last_a + upd_a, st_b * e_last_b + upd_b

    def stage_store(ck, hd, o):
        r0 = hdr + ck * c
        c_g = base + 3 * hw + hd * d
        o = o * lax.rsqrt(jnp.mean(o * o, axis=-1, keepdims=True) + EPS)
        yb = o * hnorm_ref[:, hd * d:(hd + 1) * d] * _silu(pr[r0:r0 + c, c_g:c_g + d])
        mix[ck * c:(ck + 1) * c, lw + hd * d:lw + (hd + 1) * d] = yb.astype(BF16)

    states = [state_t[hd] for hd in range(n_heads)]
    gates_all = [stage_gates(ck) for ck in range(n_chunks)]
    attn_all = [[stage_attn(stage_scores(ck, hd, *gates_all[ck]), stage_scores(ck, hd + 1, *gates_all[ck]))
                 for hd in range(0, n_heads, 2)] for ck in range(n_chunks)]
    outs_all = []
    for ck in range(n_chunks):
        outs = []
        for pi, hd in enumerate(range(0, n_heads, 2)):
            o_a, o_b, states[hd], states[hd + 1] = stage_output(attn_all[ck][pi], states[hd], states[hd + 1])
            outs += [o_a, o_b]
        outs_all.append(outs)
    for ck in range(n_chunks):
        for hd in range(n_heads):
            stage_store(ck, hd, outs_all[ck][hd])
    for hd in range(n_heads):
        state_t[hd] = states[hd]

    o_ref[...] = hprev_ref[...] + _dot(mix[...], w_out_ref[...])
    pcur[0:hdr, 0:lw] = jnp.where(g % n_t == 0, 0.0, pcur[t_blk:t_blk + hdr, 0:lw])
    pcur[hdr:, :] = pnew[...]


def _even_call(h, gain, w_in, conv_w, conv_b, wg, ba, bx, lam, lb_logits, hnorm, w_out, layer_j):
    bsz, s, d = h.shape
    lw = conv_b.shape[1]
    hw = hnorm.shape[1]
    t = MIX_BLOCK_T
    n_t = s // t
    n_blk = bsz * n_t
    in_map, out_map = _pipeline_maps(n_blk, n_t)
    return pl.pallas_call(
        functools.partial(_even_kernel, layer_j=layer_j, n_t=n_t),
        grid=(n_blk + 1,),
        in_specs=[
            pl.BlockSpec((None, t, d), in_map),
            pl.BlockSpec((None, t, d), out_map),
            _const_spec(gain.shape),
            _const_spec(w_in.shape),
            _const_spec(conv_w.shape),
            _const_spec(conv_b.shape),
            _const_spec(wg.shape),
            _const_spec(ba.shape),
            _const_spec(bx.shape),
            _const_spec(lam.shape),
            _const_spec(lb_logits.shape),
            _const_spec(hnorm.shape),
            _const_spec(w_out.shape),
        ],
        out_specs=pl.BlockSpec((None, t, d), out_map),
        out_shape=jax.ShapeDtypeStruct((bsz, s, d), F32),
        scratch_shapes=[
            pltpu.VMEM((t + SUBLANES, w_in.shape[1]), F32),
            pltpu.VMEM((t, w_in.shape[1]), F32),
            pltpu.VMEM((SUBLANES, lw), F32),
            pltpu.VMEM((hw // HGRN_HEAD_DIM, HGRN_HEAD_DIM, HGRN_HEAD_DIM), F32),
            pltpu.VMEM((t, lw + hw), BF16),
        ],
        compiler_params=pltpu.CompilerParams(
            dimension_semantics=("arbitrary",), vmem_limit_bytes=VMEM_LIMIT_BYTES),
        name="mix_even",
    )(h, h, gain, w_in, conv_w, conv_b, wg, ba, bx, lam, lb_logits, hnorm, w_out)


def _odd_kernel(h_ref, hprev_ref, gain_ref, w_in_ref, sgu_w_ref, sgu_b_ref, pool_w_ref, pool_scale_ref,
                w_out_ref, o_ref, pcur, pnew, mix, *, n_t):
    t_blk = h_ref.shape[0]
    pw = pool_scale_ref.shape[1]
    sw = (w_in_ref.shape[1] - pw) // 2
    gdim = sw // SGU_GROUPS
    pdim = pw // len(POOL_WINDOWS)
    hdr = POOL_HALO
    g = pl.program_id(0)
    blk_r = jnp.maximum(g - 1, 0)

    @pl.when(g == 0)
    def _():
        pcur[...] = jnp.zeros(pcur.shape, F32)

    xn = _rms_norm(h_ref[...], gain_ref[...]).astype(BF16)
    fillers = _projection_pieces(xn, w_in_ref, pnew)

    def fill():
        if fillers:
            fillers.pop(0)()

    fill()
    fill()

    ri = lax.broadcasted_iota(jnp.int32, (SGU_CHUNK, SGU_CHUNK), 0)
    ci = lax.broadcasted_iota(jnp.int32, (SGU_CHUNK, SGU_CHUNK), 1)
    causal = ci <= ri
    for gr in range(SGU_GROUPS):
        ln = slice(gr * gdim, (gr + 1) * gdim)
        u = _gelu_tanh(pcur[hdr:, gr * gdim:(gr + 1) * gdim])
        vg = _gelu_tanh(pcur[hdr:, sw + gr * gdim:sw + (gr + 1) * gdim])
        mu = jnp.mean(vg, axis=-1, keepdims=True)
        cen = vg - mu
        var = jnp.mean(cen * cen, axis=-1, keepdims=True)
        vn = (cen * lax.rsqrt(var + EPS)).astype(BF16)
        wm = jnp.where(causal, sgu_w_ref[gr], 0.0).astype(BF16)
        bias = sgu_b_ref[gr]
        for n in range(t_blk // SGU_CHUNK):
            rows = slice(n * SGU_CHUNK, (n + 1) * SGU_CHUNK)
            sv = _dot(wm, vn[rows, :]) + bias
            mix[rows, ln] = (u[rows, :] * sv).astype(BF16)
        fill()

    pos = (blk_r % n_t) * t_blk + lax.broadcasted_iota(jnp.int32, (t_blk, 1), 0) + 1
    for gi, win in enumerate(POOL_WINDOWS):
        ln = slice(gi * pdim, (gi + 1) * pdim)
        ext = pcur[:, 2 * sw + gi * pdim:2 * sw + (gi + 1) * pdim]
        acc = ext
        sh = 1
        while sh < win:
            acc = acc + pltpu.roll(acc, sh, 0)
            sh *= 2
        inv_count = 1.0 / jnp.minimum(pos, win).astype(F32)
        pooled = acc[hdr:, :] * inv_count - ext[hdr:, :]
        yd = _dot(pooled.astype(BF16), pool_w_ref[gi]) * pool_scale_ref[:, ln]
        mix[:, sw + gi * pdim:sw + (gi + 1) * pdim] = yd.astype(BF16)
    while fillers:
        fill()

    o_ref[...] = hprev_ref[...] + _dot(mix[...], w_out_ref[...])
    pcur[0:hdr, 2 * sw:] = jnp.where(g % n_t == 0, 0.0, pcur[t_blk:t_blk + hdr, 2 * sw:])
    pcur[hdr:, :] = pnew[...]


def _odd_call(h, gain, w_in, sgu_w, sgu_b, pool_w, pool_scale, w_out):
    bsz, s, d = h.shape
    mw = w_out.shape[0]
    t = MIX_BLOCK_T
    n_t = s // t
    n_blk = bsz * n_t
    in_map, out_map = _pipeline_maps(n_blk, n_t)
    return pl.pallas_call(
        functools.partial(_odd_kernel, n_t=n_t),
        grid=(n_blk + 1,),
        in_specs=[
            pl.BlockSpec((None, t, d), in_map),
            pl.BlockSpec((None, t, d), out_map),
            _const_spec(gain.shape),
            _const_spec(w_in.shape),
            _const_spec(sgu_w.shape),
            _const_spec(sgu_b.shape),
            _const_spec(pool_w.shape),
            _const_spec(pool_scale.shape),
            _const_spec(w_out.shape),
        ],
        out_specs=pl.BlockSpec((None, t, d), out_map),
        out_shape=jax.ShapeDtypeStruct((bsz, s, d), F32),
        scratch_shapes=[
            pltpu.VMEM((t + POOL_HALO, w_in.shape[1]), F32),
            pltpu.VMEM((t, w_in.shape[1]), F32),
            pltpu.VMEM((t, mw), BF16),
        ],
        compiler_params=pltpu.CompilerParams(
            dimension_semantics=("arbitrary",), vmem_limit_bytes=VMEM_LIMIT_BYTES),
        name="mix_odd",
    )(h, h, gain, w_in, sgu_w, sgu_b, pool_w, pool_scale, w_out)


def _gate_block_diag(wa, wx):
    n_heads, hd, _ = wa.shape
    per = LRU_GROUP // hd
    groups = n_heads // per
    out = jnp.zeros((groups, LRU_GROUP, 2 * LRU_GROUP), wa.dtype)
    for hh in range(n_heads):
        g, p = divmod(hh, per)
        out = out.at[g, p * hd:(p + 1) * hd, p * hd:(p + 1) * hd].set(wa[hh])
        out = out.at[g, p * hd:(p + 1) * hd, LRU_GROUP + p * hd:LRU_GROUP + (p + 1) * hd].set(wx[hh])
    return out


def kernel(x, norm_mix, norm_ffn, w_in_even, conv_w, conv_b, lru_wa, lru_ba, lru_wx, lru_bx, lru_lambda, hgrn_lb_logits, hgrn_norm, w_out_even, w_in_odd, sgu_w, sgu_b, pool_w, pool_scale, w_out_odd, w_ffn_in, w_ffn_out, norm_final):
    bsz, s, d = x.shape
    depth = norm_mix.shape[0]
    assert s % MIX_BLOCK_T == 0 and (bsz * s) % FFN_BLOCK_M == 0
    assert MIX_BLOCK_T % SGU_CHUNK == 0 and MIX_BLOCK_T % HGRN_CHUNK == 0
    row = lambda a: a.reshape(1, -1)
    h = x
    for layer in range(depth):
        j = layer // 2
        if layer % 2 == 0:
            wg = _gate_block_diag(lru_wa[j], lru_wx[j]).astype(BF16)
            h = _even_call(h, row(norm_mix[layer]), w_in_even[j].astype(BF16), conv_w[j], row(conv_b[j]),
                           wg, row(lru_ba[j]), row(lru_bx[j]), row(lru_lambda[j]), hgrn_lb_logits,
                           row(hgrn_norm[j]), w_out_even[j].astype(BF16), j)
        else:
            h = _odd_call(h, row(norm_mix[layer]), w_in_odd[j].astype(BF16), sgu_w[j],
                          sgu_b[j][..., None], pool_w[j].astype(BF16), row(pool_scale[j]),
                          w_out_odd[j].astype(BF16))
        h = _ffn_call(h.reshape(bsz * s, d), row(norm_ffn[layer]), w_ffn_in[layer].astype(BF16),
                      w_ffn_out[layer].astype(BF16), row(norm_final),
                      final_norm=(layer == depth - 1)).reshape(bsz, s, d)
    return h
```

```python
import functools

import jax
import jax.numpy as jnp
from jax import lax
from jax.experimental import pallas as pl
from jax.experimental.pallas import tpu as pltpu

F32 = jnp.float32
BF16 = jnp.bfloat16

EPS = 1e-6
LRU_C = 8.0
CONV_WIDTH = 4
LRU_HEADS = 8
LRU_GROUP = 256
HGRN_HEAD_DIM = 128
HGRN_CHUNK = 64
HGRN_SUB = 16
SGU_CHUNK = 128
SGU_GROUPS = 4
POOL_WINDOWS = (2, 4, 8, 16)
POOL_HALO = 16
SUBLANES = 8

MIX_BLOCK_T = 512
FFN_BLOCK_M = 1024
FFN_CHUNK = 1024
MXU_PIECE = 256
VMEM_LIMIT_BYTES = 56 * 1024 * 1024


def _dot(a, b):
    return jnp.dot(a, b, preferred_element_type=F32)


def _dot_nt(a, b):
    return lax.dot_general(a, b, (((1,), (1,)), ((), ())), preferred_element_type=F32)


def _dot_tn(a, b):
    return lax.dot_general(a, b, (((0,), (0,)), ((), ())), preferred_element_type=F32)


def _rms_norm(x, gain):
    ms = jnp.mean(x * x, axis=-1, keepdims=True)
    return x * lax.rsqrt(ms + EPS) * gain


def _sigmoid(x):
    return 0.5 * jnp.tanh(0.5 * x) + 0.5


def _sigmoid_small_tail(x):
    e = jnp.exp(-jnp.abs(x))
    big = 1.0 / (1.0 + e)
    return jnp.where(x >= 0, big, e * big)


def _silu(x):
    hx = 0.5 * x
    return hx * jnp.tanh(hx) + hx


def _gelu_tanh(x):
    c = 0.7978845608028654
    return 0.5 * x * (1.0 + jnp.tanh(c * (x + 0.044715 * (x * x * x))))


def _split2_bf16(x):
    hi = x.astype(BF16)
    lo = (x - hi.astype(F32)).astype(BF16)
    return hi, lo


def _const_spec(shape):
    nd = len(shape)
    return pl.BlockSpec(shape, lambda *_: (0,) * nd, pipeline_mode=pl.Buffered(1))


def _ffn_kernel(h_ref, gain_ref, w1_ref, w2_ref, gfin_ref, o_ref, *, final_norm):
    h = h_ref[...]
    hn = _rms_norm(h, gain_ref[...]).astype(BF16)
    acc = h
    d_ff = w1_ref.shape[1]
    for c in range(d_ff // FFN_CHUNK):
        cols = slice(c * FFN_CHUNK, (c + 1) * FFN_CHUNK)
        hid = _dot(hn, w1_ref[:, cols])
        hid = jnp.square(jnp.maximum(hid, 0.0)).astype(BF16)
        acc = acc + _dot(hid, w2_ref[cols, :])
    if final_norm:
        acc = _rms_norm(acc, gfin_ref[...])
    o_ref[...] = acc


def _ffn_call(h2d, gain, w1, w2, gfin, final_norm):
    m, d = h2d.shape
    d_ff = w1.shape[1]
    return pl.pallas_call(
        functools.partial(_ffn_kernel, final_norm=final_norm),
        grid=(m // FFN_BLOCK_M,),
        in_specs=[
            pl.BlockSpec((FFN_BLOCK_M, d), lambda i: (i, 0)),
            _const_spec((1, d)),
            _const_spec((d, d_ff)),
            _const_spec((d_ff, d)),
            _const_spec((1, d)),
        ],
        out_specs=pl.BlockSpec((FFN_BLOCK_M, d), lambda i: (i, 0)),
        out_shape=jax.ShapeDtypeStruct((m, d), F32),
        compiler_params=pltpu.CompilerParams(
            dimension_semantics=("parallel",), vmem_limit_bytes=VMEM_LIMIT_BYTES),
        name="ffn_final" if final_norm else "ffn",
    )(h2d, gain, w1, w2, gfin)


def _pipeline_maps(n_blk, n_t):
    def in_map(g):
        blk = jnp.minimum(g, n_blk - 1)
        return (blk // n_t, blk % n_t, 0)

    def out_map(g):
        blk = jnp.maximum(g - 1, 0)
        return (blk // n_t, blk % n_t, 0)

    return in_map, out_map


def _projection_pieces(xn, w_in_ref, pnew):
    def make(c0):
        def piece():
            pnew[:, c0:c0 + MXU_PIECE] = _dot(xn, w_in_ref[:, c0:c0 + MXU_PIECE])
        return piece

    return [make(c0) for c0 in range(0, w_in_ref.shape[1], MXU_PIECE)]


def _lru_scan(a, b, carry):
    t, w = a.shape
    g = t // SUBLANES
    a3 = a.reshape(g, SUBLANES, w)
    b3 = b.reshape(g, SUBLANES, w)
    row = lax.broadcasted_iota(jnp.int32, (g, SUBLANES, w), 1)
    sh = 1
    while sh < SUBLANES:
        keep = row >= sh
        a_prev = jnp.where(keep, pltpu.roll(a3, sh, 1), 1.0)
        b_prev = jnp.where(keep, pltpu.roll(b3, sh, 1), 0.0)
        b3 = a3 * b_prev + b3
        a3 = a3 * a_prev
        sh *= 2

    def chain():
        c = carry
        outs = []
        for i in range(g):
            hblk = a3[i] * c + b3[i]
            outs.append(hblk)
            c = hblk[SUBLANES - 1:SUBLANES, :]
        return jnp.concatenate(outs, axis=0), c

    return chain


def _block_diag2(a, b):
    za = jnp.zeros(a.shape, a.dtype)
    zb = jnp.zeros(b.shape, b.dtype)
    return jnp.concatenate([jnp.concatenate([a, zb], axis=1), jnp.concatenate([za, b], axis=1)], axis=0)


def _even_kernel(h_ref, hprev_ref, gain_ref, w_in_ref, conv_w_ref, conv_b_ref, wg_ref, ba_ref, bx_ref,
                 lam_ref, lb_logits_ref, hnorm_ref, w_out_ref, o_ref,
                 pcur, pnew, lru_carry, state_t, mix, *, layer_j, n_t):
    t_blk = h_ref.shape[0]
    lw = conv_b_ref.shape[1]
    hw = hnorm_ref.shape[1]
    n_heads = hw // HGRN_HEAD_DIM
    hdr = SUBLANES
    g = pl.program_id(0)
    blk_r = jnp.maximum(g - 1, 0)

    @pl.when(g == 0)
    def _():
        pcur[...] = jnp.zeros(pcur.shape, F32)

    @pl.when(blk_r % n_t == 0)
    def _():
        lru_carry[...] = jnp.zeros_like(lru_carry)
        state_t[...] = jnp.zeros_like(state_t)

    xn = _rms_norm(h_ref[...], gain_ref[...]).astype(BF16)
    fillers = _projection_pieces(xn, w_in_ref, pnew)

    def fill():
        if fillers:
            fillers.pop(0)()

    pr = pcur
    fill()
    fill()
    fill()

    neg_lam = -lam_ref[...]
    softplus = jnp.maximum(neg_lam, 0.0) + jnp.log1p(jnp.exp(-jnp.abs(neg_lam)))
    log_a_scale = -LRU_C * softplus
    for gi in range(lw // LRU_GROUP):
        ln = slice(gi * LRU_GROUP, (gi + 1) * LRU_GROUP)
        ua = conv_b_ref[:, ln] + conv_w_ref[CONV_WIDTH - 1:CONV_WIDTH, ln] * pr[hdr:, ln]
        for k in range(CONV_WIDTH - 1):
            r0 = hdr - (CONV_WIDTH - 1) + k
            ua = ua + conv_w_ref[k:k + 1, ln] * pr[r0:r0 + t_blk, ln]
        gates = _dot(ua.astype(BF16), wg_ref[gi])
        fill()
        r = _sigmoid(gates[:, :LRU_GROUP] + ba_ref[:, ln])
        i_gate = _sigmoid(gates[:, LRU_GROUP:] + bx_ref[:, ln])
        log_a = log_a_scale[:, ln] * r
        a = jnp.exp(log_a)
        th = jnp.tanh(log_a)
        y2 = -2.0 * th / (1.0 - th)
        beta = jnp.where(y2 > 0.0, y2 * lax.rsqrt(y2), 0.0)
        b_term = beta * (i_gate * ua)
        chain = _lru_scan(a, b_term, lru_carry[0:1, ln])
        fill()
        hl, carry = chain()
        lru_carry[0:1, ln] = carry
        fill()
        mix[:, ln] = (hl * _gelu_tanh(pr[hdr:, lw + gi * LRU_GROUP:lw + (gi + 1) * LRU_GROUP])).astype(BF16)
        if gi == 0:
            fill()

    while len(fillers) > 2:
        fill()

    logits = lb_logits_ref[...]
    n_lb = logits.shape[0]
    mx = logits[0:1, :]
    for i in range(1, n_lb):
        mx = jnp.maximum(mx, logits[i:i + 1, :])
    es = [jnp.exp(logits[i:i + 1, :] - mx) for i in range(n_lb)]
    den = es[0]
    for e in es[1:]:
        den = den + e
    ps = [e / den for e in es]
    cum = ps[0]
    for p in ps[1:layer_j + 1]:
        cum = cum + p
    lb = cum - ps[0]
    one_m_lb = 1.0 - lb

    c = HGRN_CHUNK
    d = HGRN_HEAD_DIM
    n_sub = c // HGRN_SUB
    n_chunks = t_blk // c
    ri = lax.broadcasted_iota(jnp.int32, (c, c), 0)
    ci = lax.broadcasted_iota(jnp.int32, (c, c), 1)
    tri = (ci <= ri).astype(BF16)
    rp = lax.broadcasted_iota(jnp.int32, (c, 2 * c), 0)
    cp = lax.broadcasted_iota(jnp.int32, (c, 2 * c), 1) % c
    diag_mask = (cp <= rp) & ((rp // HGRN_SUB) == (cp // HGRN_SUB))
    zeros_sub = jnp.zeros((HGRN_SUB, d), BF16)
    base = 2 * lw

    def stage_gates(ck):
        r0 = hdr + ck * c
        f_gate = lb + one_m_lb * _sigmoid_small_tail(pr[r0:r0 + c, base + hw:base + 2 * hw])
        hi, lo = _split2_bf16(jnp.log(f_gate))
        bc = _dot(tri, jnp.concatenate([hi, lo], axis=1))
        return 1.0 - f_gate, bc

    def stage_scores(ck, hd, k_all, bc):
        r0 = hdr + ck * c
        ln = slice(hd * d, (hd + 1) * d)
        c_q = base + hd * d
        bcum = bc[:, ln] + bc[:, hw + hd * d:hw + (hd + 1) * d]
        kk = k_all[:, ln]
        qf = _silu(pr[r0:r0 + c, c_q:c_q + d])
        vv = pr[r0:r0 + c, c_q + 2 * hw:c_q + 2 * hw + d].astype(BF16)
        m = [jnp.zeros((1, d), F32)] + [bcum[(i + 1) * HGRN_SUB - 1:(i + 1) * HGRN_SUB, :] for i in range(n_sub)]
        start = jnp.concatenate([jnp.broadcast_to(m[i], (HGRN_SUB, d)) for i in range(n_sub)], axis=0)
        dec = jnp.exp(bcum - start)
        q_d = qf * dec
        k_d = kk * (1.0 / dec)
        q_sub = [q_d[i * HGRN_SUB:(i + 1) * HGRN_SUB, :] for i in range(n_sub)]
        k_sub = [k_d[i * HGRN_SUB:(i + 1) * HGRN_SUB, :] * jnp.exp(m[i + 1] - m[i]) for i in range(n_sub)]
        q_inter = jnp.concatenate([(q_sub[i] * jnp.exp(m[i])).astype(BF16) for i in range(n_sub)], axis=0)
        k_dec = jnp.concatenate(
            [(k_sub[i] * jnp.exp(m[n_sub] - m[i + 1])).astype(BF16) for i in range(n_sub)], axis=0)
        q_off = jnp.concatenate([
            jnp.concatenate([zeros_sub if i <= j else (q_sub[i] * jnp.exp(m[i] - m[j + 1])).astype(BF16)
                             for i in range(n_sub)], axis=0)
            for j in range(n_sub - 1)], axis=1)
        k_off = jnp.concatenate([
            jnp.concatenate([k_sub[i].astype(BF16) if i == j else zeros_sub for i in range(n_sub)], axis=0)
            for j in range(n_sub - 1)], axis=1)
        return q_off, k_off, q_d.astype(BF16), k_d.astype(BF16), k_dec, q_inter, vv, jnp.exp(m[n_sub])

    def stage_attn(sa, sb):
        q_off_a, k_off_a, q_d_a, k_d_a, k_dec_a, q_inter_a, vv_a, e_last_a = sa
        q_off_b, k_off_b, q_d_b, k_d_b, k_dec_b, q_inter_b, vv_b, e_last_b = sb
        a_off = _dot_nt(jnp.concatenate([q_off_a, q_off_b], axis=1), _block_diag2(k_off_a, k_off_b))
        a_diag = _dot_nt(jnp.concatenate([q_d_a, q_d_b], axis=1), _block_diag2(k_d_a, k_d_b))
        upd_a = _dot_tn(vv_a, k_dec_a)
        upd_b = _dot_tn(vv_b, k_dec_b)
        q_inter = jnp.concatenate([q_inter_a, q_inter_b], axis=1)
        return a_off, a_diag, upd_a, upd_b, q_inter, _block_diag2(vv_a, vv_b), e_last_a, e_last_b

    def stage_output(at, st_a, st_b):
        a_off, a_diag, upd_a, upd_b, q_inter, vv2, e_last_a, e_last_b = at
        attn = a_off + jnp.where(diag_mask, a_diag, 0.0)
        o = _dot_nt(q_inter, _block_diag2(st_a.astype(BF16), st_b.astype(BF16)))
        o = o + _dot(attn.astype(BF16), vv2)
        return o[:, 0:d], o[:, d:2 * d], st_a * e_last_a + upd_a, st_b * e_last_b + upd_b

    def stage_store(ck, hd, o):
        r0 = hdr + ck * c
        c_g = base + 3 * hw + hd * d
        o = o * lax.rsqrt(jnp.mean(o * o, axis=-1, keepdims=True) + EPS)
        yb = o * hnorm_ref[:, hd * d:(hd + 1) * d] * _silu(pr[r0:r0 + c, c_g:c_g + d])
        mix[ck * c:(ck + 1) * c, lw + hd * d:lw + (hd + 1) * d] = yb.astype(BF16)

    states = [state_t[hd] for hd in range(n_heads)]
    gates_all = [stage_gates(ck) for ck in range(n_chunks)]
    while fillers:
        fill()
    attn_all = [[stage_attn(stage_scores(ck, hd, *gates_all[ck]), stage_scores(ck, hd + 1, *gates_all[ck]))
                 for hd in range(0, n_heads, 2)] for ck in range(n_chunks)]
    outs_all = []
    for ck in range(n_chunks):
        outs = []
        for pi, hd in enumerate(range(0, n_heads, 2)):
            o_a, o_b, states[hd], states[hd + 1] = stage_output(attn_all[ck][pi], states[hd], states[hd + 1])
            outs += [o_a, o_b]
        outs_all.append(outs)
    for ck in range(n_chunks):
        for hd in range(n_heads):
            stage_store(ck, hd, outs_all[ck][hd])
    for hd in range(n_heads):
        state_t[hd] = states[hd]

    o_ref[...] = hprev_ref[...] + _dot(mix[...], w_out_ref[...])
    pcur[0:hdr, 0:lw] = jnp.where(g % n_t == 0, 0.0, pcur[t_blk:t_blk + hdr, 0:lw])
    pcur[hdr:, :] = pnew[...]


def _even_call(h, gain, w_in, conv_w, conv_b, wg, ba, bx, lam, lb_logits, hnorm, w_out, layer_j):
    bsz, s, d = h.shape
    lw = conv_b.shape[1]
    hw = hnorm.shape[1]
    t = MIX_BLOCK_T
    n_t = s // t
    n_blk = bsz * n_t
    in_map, out_map = _pipeline_maps(n_blk, n_t)
    return pl.pallas_call(
        functools.partial(_even_kernel, layer_j=layer_j, n_t=n_t),
        grid=(n_blk + 1,),
        in_specs=[
            pl.BlockSpec((None, t, d), in_map),
            pl.BlockSpec((None, t, d), out_map),
            _const_spec(gain.shape),
            _const_spec(w_in.shape),
            _const_spec(conv_w.shape),
            _const_spec(conv_b.shape),
            _const_spec(wg.shape),
            _const_spec(ba.shape),
            _const_spec(bx.shape),
            _const_spec(lam.shape),
            _const_spec(lb_logits.shape),
            _const_spec(hnorm.shape),
            _const_spec(w_out.shape),
        ],
        out_specs=pl.BlockSpec((None, t, d), out_map),
        out_shape=jax.ShapeDtypeStruct((bsz, s, d), F32),
        scratch_shapes=[
            pltpu.VMEM((t + SUBLANES, w_in.shape[1]), F32),
            pltpu.VMEM((t, w_in.shape[1]), F32),
            pltpu.VMEM((SUBLANES, lw), F32),
            pltpu.VMEM((hw // HGRN_HEAD_DIM, HGRN_HEAD_DIM, HGRN_HEAD_DIM), F32),
            pltpu.VMEM((t, lw + hw), BF16),
        ],
        compiler_params=pltpu.CompilerParams(
            dimension_semantics=("arbitrary",), vmem_limit_bytes=VMEM_LIMIT_BYTES),
        name="mix_even",
    )(h, h, gain, w_in, conv_w, conv_b, wg, ba, bx, lam, lb_logits, hnorm, w_out)


def _odd_kernel(h_ref, hprev_ref, gain_ref, w_in_ref, sgu_w_ref, sgu_b_ref, pool_w_ref, pool_scale_ref,
                w_out_ref, o_ref, pcur, pnew, mix, *, n_t):
    t_blk = h_ref.shape[0]
    pw = pool_scale_ref.shape[1]
    sw = (w_in_ref.shape[1] - pw) // 2
    gdim = sw // SGU_GROUPS
    pdim = pw // len(POOL_WINDOWS)
    hdr = POOL_HALO
    g = pl.program_id(0)
    blk_r = jnp.maximum(g - 1, 0)

    @pl.when(g == 0)
    def _():
        pcur[...] = jnp.zeros(pcur.shape, F32)

    xn = _rms_norm(h_ref[...], gain_ref[...]).astype(BF16)
    fillers = _projection_pieces(xn, w_in_ref, pnew)

    def fill():
        if fillers:
            fillers.pop(0)()

    fill()
    fill()

    ri = lax.broadcasted_iota(jnp.int32, (SGU_CHUNK, SGU_CHUNK), 0)
    ci = lax.broadcasted_iota(jnp.int32, (SGU_CHUNK, SGU_CHUNK), 1)
    causal = ci <= ri
    for gr in range(SGU_GROUPS):
        ln = slice(gr * gdim, (gr + 1) * gdim)
        u = _gelu_tanh(pcur[hdr:, gr * gdim:(gr + 1) * gdim])
        vg = _gelu_tanh(pcur[hdr:, sw + gr * gdim:sw + (gr + 1) * gdim])
        mu = jnp.mean(vg, axis=-1, keepdims=True)
        cen = vg - mu
        var = jnp.mean(cen * cen, axis=-1, keepdims=True)
        vn = (cen * lax.rsqrt(var + EPS)).astype(BF16)
        wm = jnp.where(causal, sgu_w_ref[gr], 0.0).astype(BF16)
        bias = sgu_b_ref[gr]
        for n in range(t_blk // SGU_CHUNK):
            rows = slice(n * SGU_CHUNK, (n + 1) * SGU_CHUNK)
            sv = _dot(wm, vn[rows, :]) + bias
            mix[rows, ln] = (u[rows, :] * sv).astype(BF16)
        fill()

    pos = (blk_r % n_t) * t_blk + lax.broadcasted_iota(jnp.int32, (t_blk, 1), 0) + 1
    for gi, win in enumerate(POOL_WINDOWS):
        ln = slice(gi * pdim, (gi + 1) * pdim)
        ext = pcur[:, 2 * sw + gi * pdim:2 * sw + (gi + 1) * pdim]
        acc = ext
        sh = 1
        while sh < win:
            acc = acc + pltpu.roll(acc, sh, 0)
            sh *= 2
        inv_count = 1.0 / jnp.minimum(pos, win).astype(F32)
        pooled = acc[hdr:, :] * inv_count - ext[hdr:, :]
        yd = _dot(pooled.astype(BF16), pool_w_ref[gi]) * pool_scale_ref[:, ln]
        mix[:, sw + gi * pdim:sw + (gi + 1) * pdim] = yd.astype(BF16)
    while fillers:
        fill()

    o_ref[...] = hprev_ref[...] + _dot(mix[...], w_out_ref[...])
    pcur[0:hdr, 2 * sw:] = jnp.where(g % n_t == 0, 0.0, pcur[t_blk:t_blk + hdr, 2 * sw:])
    pcur[hdr:, :] = pnew[...]


def _odd_call(h, gain, w_in, sgu_w, sgu_b, pool_w, pool_scale, w_out):
    bsz, s, d = h.shape
    mw = w_out.shape[0]
    t = MIX_BLOCK_T
    n_t = s // t
    n_blk = bsz * n_t
    in_map, out_map = _pipeline_maps(n_blk, n_t)
    return pl.pallas_call(
        functools.partial(_odd_kernel, n_t=n_t),
        grid=(n_blk + 1,),
        in_specs=[
            pl.BlockSpec((None, t, d), in_map),
            pl.BlockSpec((None, t, d), out_map),
            _const_spec(gain.shape),
            _const_spec(w_in.shape),
            _const_spec(sgu_w.shape),
            _const_spec(sgu_b.shape),
            _const_spec(pool_w.shape),
            _const_spec(pool_scale.shape),
            _const_spec(w_out.shape),
        ],
        out_specs=pl.BlockSpec((None, t, d), out_map),
        out_shape=jax.ShapeDtypeStruct((bsz, s, d), F32),
        scratch_shapes=[
            pltpu.VMEM((t + POOL_HALO, w_in.shape[1]), F32),
            pltpu.VMEM((t, w_in.shape[1]), F32),
            pltpu.VMEM((t, mw), BF16),
        ],
        compiler_params=pltpu.CompilerParams(
            dimension_semantics=("arbitrary",), vmem_limit_bytes=VMEM_LIMIT_BYTES),
        name="mix_odd",
    )(h, h, gain, w_in, sgu_w, sgu_b, pool_w, pool_scale, w_out)


def _gate_block_diag(wa, wx):
    n_heads, hd, _ = wa.shape
    per = LRU_GROUP // hd
    groups = n_heads // per
    out = jnp.zeros((groups, LRU_GROUP, 2 * LRU_GROUP), wa.dtype)
    for hh in range(n_heads):
        g, p = divmod(hh, per)
        out = out.at[g, p * hd:(p + 1) * hd, p * hd:(p + 1) * hd].set(wa[hh])
        out = out.at[g, p * hd:(p + 1) * hd, LRU_GROUP + p * hd:LRU_GROUP + (p + 1) * hd].set(wx[hh])
    return out


def kernel(x, norm_mix, norm_ffn, w_in_even, conv_w, conv_b, lru_wa, lru_ba, lru_wx, lru_bx, lru_lambda, hgrn_lb_logits, hgrn_norm, w_out_even, w_in_odd, sgu_w, sgu_b, pool_w, pool_scale, w_out_odd, w_ffn_in, w_ffn_out, norm_final):
    bsz, s, d = x.shape
    depth = norm_mix.shape[0]
    assert s % MIX_BLOCK_T == 0 and (bsz * s) % FFN_BLOCK_M == 0
    assert MIX_BLOCK_T % SGU_CHUNK == 0 and MIX_BLOCK_T % HGRN_CHUNK == 0
    row = lambda a: a.reshape(1, -1)
    h = x
    for layer in range(depth):
        j = layer // 2
        if layer % 2 == 0:
            wg = _gate_block_diag(lru_wa[j], lru_wx[j]).astype(BF16)
            h = _even_call(h, row(norm_mix[layer]), w_in_even[j].astype(BF16), conv_w[j], row(conv_b[j]),
                           wg, row(lru_ba[j]), row(lru_bx[j]), row(lru_lambda[j]), hgrn_lb_logits,
                           row(hgrn_norm[j]), w_out_even[j].astype(BF16), j)
        else:
            h = _odd_call(h, row(norm_mix[layer]), w_in_odd[j].astype(BF16), sgu_w[j],
                          sgu_b[j][..., None], pool_w[j].astype(BF16), row(pool_scale[j]),
                          w_out_odd[j].astype(BF16))
        h = _ffn_call(h.reshape(bsz * s, d), row(norm_ffn[layer]), w_ffn_in[layer].astype(BF16),
                      w_ffn_out[layer].astype(BF16), row(norm_final),
                      final_norm=(layer == depth - 1)).reshape(bsz, s, d)
    return h
```

```python
import functools

import jax
import jax.numpy as jnp
from jax import lax
from jax.experimental import pallas as pl
from jax.experimental.pallas import tpu as pltpu

F32 = jnp.float32
BF16 = jnp.bfloat16

EPS = 1e-6
LRU_C = 8.0
CONV_WIDTH = 4
LRU_HEADS = 8
LRU_GROUP = 256
HGRN_HEAD_DIM = 128
HGRN_CHUNK = 64
HGRN_SUB = 16
HGRN_STAGE_CHUNKS = 2
SGU_CHUNK = 128
SGU_GROUPS = 4
POOL_WINDOWS = (2, 4, 8, 16)
POOL_HALO = 16
SUBLANES = 8

MIX_BLOCK_T = 512
FFN_BLOCK_M = 1024
FFN_CHUNK = 1024
MXU_PIECE = 256
VMEM_LIMIT_BYTES = 56 * 1024 * 1024


def _dot(a, b):
    return jnp.dot(a, b, preferred_element_type=F32)


def _dot_nt(a, b):
    return lax.dot_general(a, b, (((1,), (1,)), ((), ())), preferred_element_type=F32)


def _dot_tn(a, b):
    return lax.dot_general(a, b, (((0,), (0,)), ((), ())), preferred_element_type=F32)


def _rms_norm(x, gain):
    ms = jnp.mean(x * x, axis=-1, keepdims=True)
    return x * lax.rsqrt(ms + EPS) * gain


def _sigmoid(x):
    return 0.5 * jnp.tanh(0.5 * x) + 0.5


def _sigmoid_small_tail(x):
    e = jnp.exp(-jnp.abs(x))
    big = 1.0 / (1.0 + e)
    return jnp.where(x >= 0, big, e * big)


def _silu(x):
    hx = 0.5 * x
    return hx * jnp.tanh(hx) + hx


def _gelu_tanh(x):
    c = 0.7978845608028654
    return 0.5 * x * (1.0 + jnp.tanh(c * (x + 0.044715 * (x * x * x))))


def _split2_bf16(x):
    hi = x.astype(BF16)
    lo = (x - hi.astype(F32)).astype(BF16)
    return hi, lo


def _const_spec(shape):
    nd = len(shape)
    return pl.BlockSpec(shape, lambda *_: (0,) * nd, pipeline_mode=pl.Buffered(1))


def _layer_spec(stacked, layer):
    tail = stacked.shape[1:]
    return pl.BlockSpec((None,) + tail, lambda *_: (layer,) + (0,) * len(tail), pipeline_mode=pl.Buffered(1))


def _ffn_kernel(h_ref, gain_ref, w1_ref, w2_ref, gfin_ref, o_ref, *, final_norm):
    h = h_ref[...]
    hn = _rms_norm(h, gain_ref[...]).astype(BF16)
    acc = h
    d_ff = w1_ref.shape[1]
    for c in range(d_ff // FFN_CHUNK):
        cols = slice(c * FFN_CHUNK, (c + 1) * FFN_CHUNK)
        hid = _dot(hn, w1_ref[:, cols])
        hid = jnp.square(jnp.maximum(hid, 0.0)).astype(BF16)
        acc = acc + _dot(hid, w2_ref[cols, :])
    if final_norm:
        acc = _rms_norm(acc, gfin_ref[...])
    o_ref[...] = acc


def _ffn_call(h2d, gain, w1_all, w2_all, layer, gfin, final_norm):
    m, d = h2d.shape
    return pl.pallas_call(
        functools.partial(_ffn_kernel, final_norm=final_norm),
        grid=(m // FFN_BLOCK_M,),
        in_specs=[
            pl.BlockSpec((FFN_BLOCK_M, d), lambda i: (i, 0)),
            _const_spec((1, d)),
            _layer_spec(w1_all, layer),
            _layer_spec(w2_all, layer),
            _const_spec((1, d)),
        ],
        out_specs=pl.BlockSpec((FFN_BLOCK_M, d), lambda i: (i, 0)),
        out_shape=jax.ShapeDtypeStruct((m, d), F32),
        compiler_params=pltpu.CompilerParams(
            dimension_semantics=("parallel",), vmem_limit_bytes=VMEM_LIMIT_BYTES),
        name="ffn_final" if final_norm else "ffn",
    )(h2d, gain, w1_all, w2_all, gfin)


def _pipeline_maps(n_blk, n_t):
    def in_map(g):
        blk = jnp.minimum(g, n_blk - 1)
        return (blk // n_t, blk % n_t, 0)

    def out_map(g):
        blk = jnp.maximum(g - 1, 0)
        return (blk // n_t, blk % n_t, 0)

    return in_map, out_map


def _projection_pieces(xn, w_in_ref, pnew):
    def make(c0):
        def piece():
            pnew[:, c0:c0 + MXU_PIECE] = _dot(xn, w_in_ref[:, c0:c0 + MXU_PIECE])
        return piece

    return [make(c0) for c0 in range(0, w_in_ref.shape[1], MXU_PIECE)]


def _lru_scan(a, b, carry):
    t, w = a.shape
    g = t // SUBLANES
    a3 = a.reshape(g, SUBLANES, w)
    b3 = b.reshape(g, SUBLANES, w)
    row = lax.broadcasted_iota(jnp.int32, (g, SUBLANES, w), 1)
    sh = 1
    while sh < SUBLANES:
        keep = row >= sh
        a_prev = jnp.where(keep, pltpu.roll(a3, sh, 1), 1.0)
        b_prev = jnp.where(keep, pltpu.roll(b3, sh, 1), 0.0)
        b3 = a3 * b_prev + b3
        a3 = a3 * a_prev
        sh *= 2

    def chain():
        c = carry
        outs = []
        for i in range(g):
            hblk = a3[i] * c + b3[i]
            outs.append(hblk)
            c = hblk[SUBLANES - 1:SUBLANES, :]
        return jnp.concatenate(outs, axis=0), c

    return chain


def _block_diag2(a, b):
    za = jnp.zeros(a.shape, a.dtype)
    zb = jnp.zeros(b.shape, b.dtype)
    return jnp.concatenate([jnp.concatenate([a, zb], axis=1), jnp.concatenate([za, b], axis=1)], axis=0)


def _even_kernel(h_ref, hprev_ref, gain_ref, w_in_ref, conv_w_ref, conv_b_ref, wg_ref, ba_ref, bx_ref,
                 lam_ref, lb_logits_ref, hnorm_ref, w_out_ref, o_ref,
                 pcur, pnew, lru_carry, state_t, mix, *, layer_j, n_t):
    t_blk = h_ref.shape[0]
    lw = conv_b_ref.shape[1]
    hw = hnorm_ref.shape[1]
    n_heads = hw // HGRN_HEAD_DIM
    hdr = SUBLANES
    g = pl.program_id(0)
    blk_r = jnp.maximum(g - 1, 0)

    @pl.when(g == 0)
    def _():
        pcur[...] = jnp.zeros(pcur.shape, F32)

    @pl.when(blk_r % n_t == 0)
    def _():
        lru_carry[...] = jnp.zeros_like(lru_carry)
        state_t[...] = jnp.zeros_like(state_t)

    xn = _rms_norm(h_ref[...], gain_ref[...]).astype(BF16)
    fillers = _projection_pieces(xn, w_in_ref, pnew)

    def fill():
        if fillers:
            fillers.pop(0)()

    pr = pcur
    fill()
    fill()
    fill()

    neg_lam = -lam_ref[...]
    softplus = jnp.maximum(neg_lam, 0.0) + jnp.log1p(jnp.exp(-jnp.abs(neg_lam)))
    log_a_scale = -LRU_C * softplus
    for gi in range(lw // LRU_GROUP):
        ln = slice(gi * LRU_GROUP, (gi + 1) * LRU_GROUP)
        ua = conv_b_ref[:, ln] + conv_w_ref[CONV_WIDTH - 1:CONV_WIDTH, ln] * pr[hdr:, ln]
        for k in range(CONV_WIDTH - 1):
            r0 = hdr - (CONV_WIDTH - 1) + k
            ua = ua + conv_w_ref[k:k + 1, ln] * pr[r0:r0 + t_blk, ln]
        gates = _dot(ua.astype(BF16), wg_ref[gi])
        fill()
        r = _sigmoid(gates[:, :LRU_GROUP] + ba_ref[:, ln])
        i_gate = _sigmoid(gates[:, LRU_GROUP:] + bx_ref[:, ln])
        log_a = log_a_scale[:, ln] * r
        a = jnp.exp(log_a)
        th = jnp.tanh(log_a)
        y2 = -2.0 * th / (1.0 - th)
        beta = jnp.where(y2 > 0.0, y2 * lax.rsqrt(y2), 0.0)
        b_term = beta * (i_gate * ua)
        chain = _lru_scan(a, b_term, lru_carry[0:1, ln])
        fill()
        hl, carry = chain()
        lru_carry[0:1, ln] = carry
        fill()
        mix[:, ln] = (hl * _gelu_tanh(pr[hdr:, lw + gi * LRU_GROUP:lw + (gi + 1) * LRU_GROUP])).astype(BF16)
        if gi == 0:
            fill()

    while len(fillers) > 2:
        fill()

    logits = lb_logits_ref[...]
    n_lb = logits.shape[0]
    mx = logits[0:1, :]
    for i in range(1, n_lb):
        mx = jnp.maximum(mx, logits[i:i + 1, :])
    es = [jnp.exp(logits[i:i + 1, :] - mx) for i in range(n_lb)]
    den = es[0]
    for e in es[1:]:
        den = den + e
    ps = [e / den for e in es]
    cum = ps[0]
    for p in ps[1:layer_j + 1]:
        cum = cum + p
    lb = cum - ps[0]
    one_m_lb = 1.0 - lb

    c = HGRN_CHUNK
    d = HGRN_HEAD_DIM
    n_sub = c // HGRN_SUB
    n_chunks = t_blk // c
    ri = lax.broadcasted_iota(jnp.int32, (c, c), 0)
    ci = lax.broadcasted_iota(jnp.int32, (c, c), 1)
    tri = (ci <= ri).astype(BF16)
    rp = lax.broadcasted_iota(jnp.int32, (c, 2 * c), 0)
    cp = lax.broadcasted_iota(jnp.int32, (c, 2 * c), 1) % c
    diag_mask = (cp <= rp) & ((rp // HGRN_SUB) == (cp // HGRN_SUB))
    zeros_sub = jnp.zeros((HGRN_SUB, d), BF16)
    base = 2 * lw

    def stage_gates(ck):
        r0 = hdr + ck * c
        f_gate = lb + one_m_lb * _sigmoid_small_tail(pr[r0:r0 + c, base + hw:base + 2 * hw])
        hi, lo = _split2_bf16(jnp.log(f_gate))
        bc = _dot(tri, jnp.concatenate([hi, lo], axis=1))
        return 1.0 - f_gate, bc

    def stage_scores(ck, hd, k_all, bc):
        r0 = hdr + ck * c
        ln = slice(hd * d, (hd + 1) * d)
        c_q = base + hd * d
        bcum = bc[:, ln] + bc[:, hw + hd * d:hw + (hd + 1) * d]
        kk = k_all[:, ln]
        qf = _silu(pr[r0:r0 + c, c_q:c_q + d])
        vv = pr[r0:r0 + c, c_q + 2 * hw:c_q + 2 * hw + d].astype(BF16)
        m = [jnp.zeros((1, d), F32)] + [bcum[(i + 1) * HGRN_SUB - 1:(i + 1) * HGRN_SUB, :] for i in range(n_sub)]
        start = jnp.concatenate([jnp.broadcast_to(m[i], (HGRN_SUB, d)) for i in range(n_sub)], axis=0)
        dec = jnp.exp(bcum - start)
        q_d = qf * dec
        k_d = kk * (1.0 / dec)
        q_sub = [q_d[i * HGRN_SUB:(i + 1) * HGRN_SUB, :] for i in range(n_sub)]
        k_sub = [k_d[i * HGRN_SUB:(i + 1) * HGRN_SUB, :] * jnp.exp(m[i + 1] - m[i]) for i in range(n_sub)]
        q_inter = jnp.concatenate([(q_sub[i] * jnp.exp(m[i])).astype(BF16) for i in range(n_sub)], axis=0)
        k_dec = jnp.concatenate(
            [(k_sub[i] * jnp.exp(m[n_sub] - m[i + 1])).astype(BF16) for i in range(n_sub)], axis=0)
        q_off = jnp.concatenate([
            jnp.concatenate([zeros_sub if i <= j else (q_sub[i] * jnp.exp(m[i] - m[j + 1])).astype(BF16)
                             for i in range(n_sub)], axis=0)
            for j in range(n_sub - 1)], axis=1)
        k_off = jnp.concatenate([
            jnp.concatenate([k_sub[i].astype(BF16) if i == j else zeros_sub for i in range(n_sub)], axis=0)
            for j in range(n_sub - 1)], axis=1)
        return q_off, k_off, q_d.astype(BF16), k_d.astype(BF16), k_dec, q_inter, vv, jnp.exp(m[n_sub])

    def stage_attn(sa, sb):
        q_off_a, k_off_a, q_d_a, k_d_a, k_dec_a, q_inter_a, vv_a, e_last_a = sa
        q_off_b, k_off_b, q_d_b, k_d_b, k_dec_b, q_inter_b, vv_b, e_last_b = sb
        a_off = _dot_nt(jnp.concatenate([q_off_a, q_off_b], axis=1), _block_diag2(k_off_a, k_off_b))
        a_diag = _dot_nt(jnp.concatenate([q_d_a, q_d_b], axis=1), _block_diag2(k_d_a, k_d_b))
        upd_a = _dot_tn(vv_a, k_dec_a)
        upd_b = _dot_tn(vv_b, k_dec_b)
        q_inter = jnp.concatenate([q_inter_a, q_inter_b], axis=1)
        return a_off, a_diag, upd_a, upd_b, q_inter, _block_diag2(vv_a, vv_b), e_last_a, e_last_b

    def stage_output(at, st_a, st_b):
        a_off, a_diag, upd_a, upd_b, q_inter, vv2, e_last_a, e_last_b = at
        attn = a_off + jnp.where(diag_mask, a_diag, 0.0)
        o = _dot_nt(q_inter, _block_diag2(st_a.astype(BF16), st_b.astype(BF16)))
        o = o + _dot(attn.astype(BF16), vv2)
        return o[:, 0:d], o[:, d:2 * d], st_a * e_last_a + upd_a, st_b * e_last_b + upd_b

    def stage_store(ck, hd, o):
        r0 = hdr + ck * c
        c_g = base + 3 * hw + hd * d
        o = o * lax.rsqrt(jnp.mean(o * o, axis=-1, keepdims=True) + EPS)
        yb = o * hnorm_ref[:, hd * d:(hd + 1) * d] * _silu(pr[r0:r0 + c, c_g:c_g + d])
        mix[ck * c:(ck + 1) * c, lw + hd * d:lw + (hd + 1) * d] = yb.astype(BF16)

    states = [state_t[hd] for hd in range(n_heads)]
    while fillers:
        fill()
    for c0 in range(0, n_chunks, HGRN_STAGE_CHUNKS):
        cks = range(c0, c0 + HGRN_STAGE_CHUNKS)
        gates_all = {ck: stage_gates(ck) for ck in cks}
        attn_all = {ck: [stage_attn(stage_scores(ck, hd, *gates_all[ck]), stage_scores(ck, hd + 1, *gates_all[ck]))
                         for hd in range(0, n_heads, 2)] for ck in cks}
        outs_all = {}
        for ck in cks:
            outs = []
            for pi, hd in enumerate(range(0, n_heads, 2)):
                o_a, o_b, states[hd], states[hd + 1] = stage_output(attn_all[ck][pi], states[hd], states[hd + 1])
                outs += [o_a, o_b]
            outs_all[ck] = outs
        for ck in cks:
            for hd in range(n_heads):
                stage_store(ck, hd, outs_all[ck][hd])
    for hd in range(n_heads):
        state_t[hd] = states[hd]

    o_ref[...] = hprev_ref[...] + _dot(mix[...], w_out_ref[...])
    pcur[0:hdr, 0:lw] = jnp.where(g % n_t == 0, 0.0, pcur[t_blk:t_blk + hdr, 0:lw])
    pcur[hdr:, :] = pnew[...]


def _even_call(h, gain, w_in, conv_w, conv_b, wg, ba, bx, lam, lb_logits, hnorm, w_out, layer_j):
    bsz, s, d = h.shape
    lw = conv_b.shape[1]
    hw = hnorm.shape[1]
    t = MIX_BLOCK_T
    n_t = s // t
    n_blk = bsz * n_t
    in_map, out_map = _pipeline_maps(n_blk, n_t)
    return pl.pallas_call(
        functools.partial(_even_kernel, layer_j=layer_j, n_t=n_t),
        grid=(n_blk + 1,),
        in_specs=[
            pl.BlockSpec((None, t, d), in_map),
            pl.BlockSpec((None, t, d), out_map),
            _const_spec(gain.shape),
            _layer_spec(w_in, layer_j),
            _const_spec(conv_w.shape),
            _const_spec(conv_b.shape),
            _const_spec(wg.shape),
            _const_spec(ba.shape),
            _const_spec(bx.shape),
            _const_spec(lam.shape),
            _const_spec(lb_logits.shape),
            _const_spec(hnorm.shape),
            _layer_spec(w_out, layer_j),
        ],
        out_specs=pl.BlockSpec((None, t, d), out_map),
        out_shape=jax.ShapeDtypeStruct((bsz, s, d), F32),
        scratch_shapes=[
            pltpu.VMEM((t + SUBLANES, w_in.shape[-1]), F32),
            pltpu.VMEM((t, w_in.shape[-1]), F32),
            pltpu.VMEM((SUBLANES, lw), F32),
            pltpu.VMEM((hw // HGRN_HEAD_DIM, HGRN_HEAD_DIM, HGRN_HEAD_DIM), F32),
            pltpu.VMEM((t, lw + hw), BF16),
        ],
        compiler_params=pltpu.CompilerParams(
            dimension_semantics=("arbitrary",), vmem_limit_bytes=VMEM_LIMIT_BYTES),
        name="mix_even",
    )(h, h, gain, w_in, conv_w, conv_b, wg, ba, bx, lam, lb_logits, hnorm, w_out)


def _odd_kernel(h_ref, hprev_ref, gain_ref, w_in_ref, sgu_w_ref, sgu_b_ref, pool_w_ref, pool_scale_ref,
                w_out_ref, o_ref, pcur, pnew, mix, *, n_t):
    t_blk = h_ref.shape[0]
    pw = pool_scale_ref.shape[1]
    sw = (w_in_ref.shape[1] - pw) // 2
    gdim = sw // SGU_GROUPS
    pdim = pw // len(POOL_WINDOWS)
    hdr = POOL_HALO
    g = pl.program_id(0)
    blk_r = jnp.maximum(g - 1, 0)

    @pl.when(g == 0)
    def _():
        pcur[...] = jnp.zeros(pcur.shape, F32)

    xn = _rms_norm(h_ref[...], gain_ref[...]).astype(BF16)
    fillers = _projection_pieces(xn, w_in_ref, pnew)

    def fill():
        if fillers:
            fillers.pop(0)()

    fill()
    fill()
    fill()

    ri = lax.broadcasted_iota(jnp.int32, (SGU_CHUNK, SGU_CHUNK), 0)
    ci = lax.broadcasted_iota(jnp.int32, (SGU_CHUNK, SGU_CHUNK), 1)
    causal = ci <= ri
    for gr in range(SGU_GROUPS):
        ln = slice(gr * gdim, (gr + 1) * gdim)
        u = _gelu_tanh(pcur[hdr:, gr * gdim:(gr + 1) * gdim])
        vg = _gelu_tanh(pcur[hdr:, sw + gr * gdim:sw + (gr + 1) * gdim])
        mu = jnp.mean(vg, axis=-1, keepdims=True)
        cen = vg - mu
        var = jnp.mean(cen * cen, axis=-1, keepdims=True)
        vn = (cen * lax.rsqrt(var + EPS)).astype(BF16)
        wm = jnp.where(causal, sgu_w_ref[gr], 0.0).astype(BF16)
        bias = sgu_b_ref[gr]
        for n in range(t_blk // SGU_CHUNK):
            rows = slice(n * SGU_CHUNK, (n + 1) * SGU_CHUNK)
            sv = _dot(wm, vn[rows, :]) + bias
            mix[rows, ln] = (u[rows, :] * sv).astype(BF16)
        fill()

    pos = (blk_r % n_t) * t_blk + lax.broadcasted_iota(jnp.int32, (t_blk, 1), 0) + 1
    for gi, win in enumerate(POOL_WINDOWS):
        ln = slice(gi * pdim, (gi + 1) * pdim)
        ext = pcur[:, 2 * sw + gi * pdim:2 * sw + (gi + 1) * pdim]
        acc = ext
        sh = 1
        while sh < win:
            acc = acc + pltpu.roll(acc, sh, 0)
            sh *= 2
        inv_count = 1.0 / jnp.minimum(pos, win).astype(F32)
        pooled = acc[hdr:, :] * inv_count - ext[hdr:, :]
        yd = _dot(pooled.astype(BF16), pool_w_ref[gi]) * pool_scale_ref[:, ln]
        mix[:, sw + gi * pdim:sw + (gi + 1) * pdim] = yd.astype(BF16)
    while fillers:
        fill()

    o_ref[...] = hprev_ref[...] + _dot(mix[...], w_out_ref[...])
    pcur[0:hdr, 2 * sw:] = jnp.where(g % n_t == 0, 0.0, pcur[t_blk:t_blk + hdr, 2 * sw:])
    pcur[hdr:, :] = pnew[...]


def _odd_call(h, gain, w_in, sgu_w, sgu_b, pool_w, pool_scale, w_out, layer_j):
    bsz, s, d = h.shape
    mw = w_out.shape[-2]
    t = MIX_BLOCK_T
    n_t = s // t
    n_blk = bsz * n_t
    in_map, out_map = _pipeline_maps(n_blk, n_t)
    return pl.pallas_call(
        functools.partial(_odd_kernel, n_t=n_t),
        grid=(n_blk + 1,),
        in_specs=[
            pl.BlockSpec((None, t, d), in_map),
            pl.BlockSpec((None, t, d), out_map),
            _const_spec(gain.shape),
            _layer_spec(w_in, layer_j),
            _const_spec(sgu_w.shape),
            _const_spec(sgu_b.shape),
            _layer_spec(pool_w, layer_j),
            _const_spec(pool_scale.shape),
            _layer_spec(w_out, layer_j),
        ],
        out_specs=pl.BlockSpec((None, t, d), out_map),
        out_shape=jax.ShapeDtypeStruct((bsz, s, d), F32),
        scratch_shapes=[
            pltpu.VMEM((t + POOL_HALO, w_in.shape[-1]), F32),
            pltpu.VMEM((t, w_in.shape[-1]), F32),
            pltpu.VMEM((t, mw), BF16),
        ],
        compiler_params=pltpu.CompilerParams(
            dimension_semantics=("arbitrary",), vmem_limit_bytes=VMEM_LIMIT_BYTES),
        name="mix_odd",
    )(h, h, gain, w_in, sgu_w, sgu_b, pool_w, pool_scale, w_out)


def _gate_block_diag(wa, wx):
    n_heads, hd, _ = wa.shape
    per = LRU_GROUP // hd
    groups = n_heads // per
    out = jnp.zeros((groups, LRU_GROUP, 2 * LRU_GROUP), wa.dtype)
    for hh in range(n_heads):
        g, p = divmod(hh, per)
        out = out.at[g, p * hd:(p + 1) * hd, p * hd:(p + 1) * hd].set(wa[hh])
        out = out.at[g, p * hd:(p + 1) * hd, LRU_GROUP + p * hd:LRU_GROUP + (p + 1) * hd].set(wx[hh])
    return out


def kernel(x, norm_mix, norm_ffn, w_in_even, conv_w, conv_b, lru_wa, lru_ba, lru_wx, lru_bx, lru_lambda, hgrn_lb_logits, hgrn_norm, w_out_even, w_in_odd, sgu_w, sgu_b, pool_w, pool_scale, w_out_odd, w_ffn_in, w_ffn_out, norm_final):
    bsz, s, d = x.shape
    depth = norm_mix.shape[0]
    assert s % MIX_BLOCK_T == 0 and (bsz * s) % FFN_BLOCK_M == 0
    assert MIX_BLOCK_T % SGU_CHUNK == 0 and MIX_BLOCK_T % HGRN_CHUNK == 0
    row = lambda a: a.reshape(1, -1)
    w_in_even, w_out_even, w_in_odd, w_out_odd, pool_w, w_ffn_in, w_ffn_out = (
        w.astype(BF16) for w in (w_in_even, w_out_even, w_in_odd, w_out_odd, pool_w, w_ffn_in, w_ffn_out))
    h = x
    for layer in range(depth):
        j = layer // 2
        if layer % 2 == 0:
            wg = _gate_block_diag(lru_wa[j], lru_wx[j]).astype(BF16)
            h = _even_call(h, row(norm_mix[layer]), w_in_even, conv_w[j], row(conv_b[j]),
                           wg, row(lru_ba[j]), row(lru_bx[j]), row(lru_lambda[j]), hgrn_lb_logits,
                           row(hgrn_norm[j]), w_out_even, j)
        else:
            h = _odd_call(h, row(norm_mix[layer]), w_in_odd, sgu_w[j], sgu_b[j][..., None], pool_w,
                          row(pool_scale[j]), w_out_odd, j)
        h = _ffn_call(h.reshape(bsz * s, d), row(norm_ffn[layer]), w_ffn_in, w_ffn_out, layer, row(norm_final),
                      final_norm=(layer == depth - 1)).reshape(bsz, s, d)
    return h
```

```python
import functools

import jax
import jax.numpy as jnp
from jax import lax
from jax.experimental import pallas as pl
from jax.experimental.pallas import tpu as pltpu

F32 = jnp.float32
BF16 = jnp.bfloat16

EPS = 1e-6
LRU_C = 8.0
CONV_WIDTH = 4
LRU_HEADS = 8
LRU_GROUP = 256
HGRN_HEAD_DIM = 128
HGRN_CHUNK = 64
HGRN_SUB = 16
HGRN_STAGE_CHUNKS = 2
SGU_CHUNK = 128
SGU_GROUPS = 4
POOL_WINDOWS = (2, 4, 8, 16)
POOL_HALO = 16
SUBLANES = 8

MIX_BLOCK_T = 512
FFN_BLOCK_M = 1024
FFN_CHUNK = 1024
MXU_PIECE = 256
VMEM_LIMIT_BYTES = 56 * 1024 * 1024


def _dot(a, b):
    return jnp.dot(a, b, preferred_element_type=F32)


def _dot_nt(a, b):
    return lax.dot_general(a, b, (((1,), (1,)), ((), ())), preferred_element_type=F32)


def _dot_tn(a, b):
    return lax.dot_general(a, b, (((0,), (0,)), ((), ())), preferred_element_type=F32)


def _rms_norm(x, gain):
    ms = jnp.mean(x * x, axis=-1, keepdims=True)
    return x * lax.rsqrt(ms + EPS) * gain


def _sigmoid(x):
    return 0.5 * jnp.tanh(0.5 * x) + 0.5


def _sigmoid_small_tail(x):
    e = jnp.exp(-jnp.abs(x))
    big = 1.0 / (1.0 + e)
    return jnp.where(x >= 0, big, e * big)


def _silu(x):
    hx = 0.5 * x
    return hx * jnp.tanh(hx) + hx


def _gelu_tanh(x):
    c = 0.7978845608028654
    return 0.5 * x * (1.0 + jnp.tanh(c * (x + 0.044715 * (x * x * x))))


def _split2_bf16(x):
    hi = x.astype(BF16)
    lo = (x - hi.astype(F32)).astype(BF16)
    return hi, lo


def _const_spec(shape):
    nd = len(shape)
    return pl.BlockSpec(shape, lambda *_: (0,) * nd, pipeline_mode=pl.Buffered(1))


def _layer_spec(stacked, layer):
    tail = stacked.shape[1:]
    return pl.BlockSpec((None,) + tail, lambda *_: (layer,) + (0,) * len(tail), pipeline_mode=pl.Buffered(1))


def _ffn_kernel(h_ref, gain_ref, w1_ref, w2_ref, gfin_ref, o_ref, *, final_norm):
    h = h_ref[...]
    hn = _rms_norm(h, gain_ref[...]).astype(BF16)
    acc = h
    d_ff = w1_ref.shape[1]
    for c in range(d_ff // FFN_CHUNK):
        cols = slice(c * FFN_CHUNK, (c + 1) * FFN_CHUNK)
        hid = _dot(hn, w1_ref[:, cols])
        hid = jnp.square(jnp.maximum(hid, 0.0)).astype(BF16)
        acc = acc + _dot(hid, w2_ref[cols, :])
    if final_norm:
        acc = _rms_norm(acc, gfin_ref[...])
    o_ref[...] = acc


def _ffn_call(h2d, gain, w1_all, w2_all, layer, gfin, final_norm):
    m, d = h2d.shape
    return pl.pallas_call(
        functools.partial(_ffn_kernel, final_norm=final_norm),
        grid=(m // FFN_BLOCK_M,),
        in_specs=[
            pl.BlockSpec((FFN_BLOCK_M, d), lambda i: (i, 0)),
            _const_spec((1, d)),
            _layer_spec(w1_all, layer),
            _layer_spec(w2_all, layer),
            _const_spec((1, d)),
        ],
        out_specs=pl.BlockSpec((FFN_BLOCK_M, d), lambda i: (i, 0)),
        out_shape=jax.ShapeDtypeStruct((m, d), F32),
        compiler_params=pltpu.CompilerParams(
            dimension_semantics=("parallel",), vmem_limit_bytes=VMEM_LIMIT_BYTES),
        name="ffn_final" if final_norm else "ffn",
    )(h2d, gain, w1_all, w2_all, gfin)


def _pipeline_maps(n_blk, n_t):
    def in_map(g):
        blk = jnp.minimum(g, n_blk - 1)
        return (blk // n_t, blk % n_t, 0)

    def out_map(g):
        blk = jnp.maximum(g - 1, 0)
        return (blk // n_t, blk % n_t, 0)

    return in_map, out_map


def _projection_pieces(xn, w_in_ref, pnew):
    def make(c0):
        def piece():
            pnew[:, c0:c0 + MXU_PIECE] = _dot(xn, w_in_ref[:, c0:c0 + MXU_PIECE])
        return piece

    return [make(c0) for c0 in range(0, w_in_ref.shape[1], MXU_PIECE)]


def _lru_scan(a, b, carry):
    t, w = a.shape
    g = t // SUBLANES
    a3 = a.reshape(g, SUBLANES, w)
    b3 = b.reshape(g, SUBLANES, w)
    row = lax.broadcasted_iota(jnp.int32, (g, SUBLANES, w), 1)
    sh = 1
    while sh < SUBLANES:
        keep = row >= sh
        a_prev = jnp.where(keep, pltpu.roll(a3, sh, 1), 1.0)
        b_prev = jnp.where(keep, pltpu.roll(b3, sh, 1), 0.0)
        b3 = a3 * b_prev + b3
        a3 = a3 * a_prev
        sh *= 2

    def chain():
        c = carry
        outs = []
        for i in range(g):
            hblk = a3[i] * c + b3[i]
            outs.append(hblk)
            c = hblk[SUBLANES - 1:SUBLANES, :]
        return jnp.concatenate(outs, axis=0), c

    return chain


def _block_diag2(a, b):
    za = jnp.zeros(a.shape, a.dtype)
    zb = jnp.zeros(b.shape, b.dtype)
    return jnp.concatenate([jnp.concatenate([a, zb], axis=1), jnp.concatenate([za, b], axis=1)], axis=0)


def _even_kernel(h_ref, gain_ref, w_in_ref, conv_w_ref, conv_b_ref, wg_ref, ba_ref, bx_ref,
                 lam_ref, lb_logits_ref, hnorm_ref, w_out_ref, o_ref,
                 pcur, pnew, lru_carry, state_t, mix, hres, *, layer_j, n_t):
    t_blk = h_ref.shape[0]
    lw = conv_b_ref.shape[1]
    hw = hnorm_ref.shape[1]
    n_heads = hw // HGRN_HEAD_DIM
    hdr = SUBLANES
    g = pl.program_id(0)
    blk_r = jnp.maximum(g - 1, 0)

    @pl.when(g == 0)
    def _():
        pcur[...] = jnp.zeros(pcur.shape, F32)
        hres[...] = jnp.zeros(hres.shape, F32)

    @pl.when(blk_r % n_t == 0)
    def _():
        lru_carry[...] = jnp.zeros_like(lru_carry)
        state_t[...] = jnp.zeros_like(state_t)

    xn = _rms_norm(h_ref[...], gain_ref[...]).astype(BF16)
    fillers = _projection_pieces(xn, w_in_ref, pnew)

    def fill():
        if fillers:
            fillers.pop(0)()

    pr = pcur
    fill()
    fill()
    fill()

    neg_lam = -lam_ref[...]
    softplus = jnp.maximum(neg_lam, 0.0) + jnp.log1p(jnp.exp(-jnp.abs(neg_lam)))
    log_a_scale = -LRU_C * softplus
    for gi in range(lw // LRU_GROUP):
        ln = slice(gi * LRU_GROUP, (gi + 1) * LRU_GROUP)
        ua = conv_b_ref[:, ln] + conv_w_ref[CONV_WIDTH - 1:CONV_WIDTH, ln] * pr[hdr:, ln]
        for k in range(CONV_WIDTH - 1):
            r0 = hdr - (CONV_WIDTH - 1) + k
            ua = ua + conv_w_ref[k:k + 1, ln] * pr[r0:r0 + t_blk, ln]
        gates = _dot(ua.astype(BF16), wg_ref[gi])
        fill()
        r = _sigmoid(gates[:, :LRU_GROUP] + ba_ref[:, ln])
        i_gate = _sigmoid(gates[:, LRU_GROUP:] + bx_ref[:, ln])
        log_a = log_a_scale[:, ln] * r
        a = jnp.exp(log_a)
        th = jnp.tanh(log_a)
        y2 = -2.0 * th / (1.0 - th)
        beta = jnp.where(y2 > 0.0, y2 * lax.rsqrt(y2), 0.0)
        b_term = beta * (i_gate * ua)
        chain = _lru_scan(a, b_term, lru_carry[0:1, ln])
        fill()
        hl, carry = chain()
        lru_carry[0:1, ln] = carry
        fill()
        mix[:, ln] = (hl * _gelu_tanh(pr[hdr:, lw + gi * LRU_GROUP:lw + (gi + 1) * LRU_GROUP])).astype(BF16)
        if gi == 0:
            fill()

    while len(fillers) > 2:
        fill()

    logits = lb_logits_ref[...]
    n_lb = logits.shape[0]
    mx = logits[0:1, :]
    for i in range(1, n_lb):
        mx = jnp.maximum(mx, logits[i:i + 1, :])
    es = [jnp.exp(logits[i:i + 1, :] - mx) for i in range(n_lb)]
    den = es[0]
    for e in es[1:]:
        den = den + e
    ps = [e / den for e in es]
    cum = ps[0]
    for p in ps[1:layer_j + 1]:
        cum = cum + p
    lb = cum - ps[0]
    one_m_lb = 1.0 - lb

    c = HGRN_CHUNK
    d = HGRN_HEAD_DIM
    n_sub = c // HGRN_SUB
    n_chunks = t_blk // c
    ri = lax.broadcasted_iota(jnp.int32, (c, c), 0)
    ci = lax.broadcasted_iota(jnp.int32, (c, c), 1)
    tri = (ci <= ri).astype(BF16)
    rp = lax.broadcasted_iota(jnp.int32, (c, 2 * c), 0)
    cp = lax.broadcasted_iota(jnp.int32, (c, 2 * c), 1) % c
    diag_mask = (cp <= rp) & ((rp // HGRN_SUB) == (cp // HGRN_SUB))
    zeros_sub = jnp.zeros((HGRN_SUB, d), BF16)
    base = 2 * lw

    def stage_gates(ck):
        r0 = hdr + ck * c
        f_gate = lb + one_m_lb * _sigmoid_small_tail(pr[r0:r0 + c, base + hw:base + 2 * hw])
        hi, lo = _split2_bf16(jnp.log(f_gate))
        bc = _dot(tri, jnp.concatenate([hi, lo], axis=1))
        return 1.0 - f_gate, bc

    def stage_scores(ck, hd, k_all, bc):
        r0 = hdr + ck * c
        ln = slice(hd * d, (hd + 1) * d)
        c_q = base + hd * d
        bcum = bc[:, ln] + bc[:, hw + hd * d:hw + (hd + 1) * d]
        kk = k_all[:, ln]
        qf = _silu(pr[r0:r0 + c, c_q:c_q + d])
        vv = pr[r0:r0 + c, c_q + 2 * hw:c_q + 2 * hw + d].astype(BF16)
        m = [jnp.zeros((1, d), F32)] + [bcum[(i + 1) * HGRN_SUB - 1:(i + 1) * HGRN_SUB, :] for i in range(n_sub)]
        start = jnp.concatenate([jnp.broadcast_to(m[i], (HGRN_SUB, d)) for i in range(n_sub)], axis=0)
        dec = jnp.exp(bcum - start)
        q_d = qf * dec
        k_d = kk * (1.0 / dec)
        q_sub = [q_d[i * HGRN_SUB:(i + 1) * HGRN_SUB, :] for i in range(n_sub)]
        k_sub = [k_d[i * HGRN_SUB:(i + 1) * HGRN_SUB, :] * jnp.exp(m[i + 1] - m[i]) for i in range(n_sub)]
        q_inter = jnp.concatenate([(q_sub[i] * jnp.exp(m[i])).astype(BF16) for i in range(n_sub)], axis=0)
        k_dec = jnp.concatenate(
            [(k_sub[i] * jnp.exp(m[n_sub] - m[i + 1])).astype(BF16) for i in range(n_sub)], axis=0)
        q_off = jnp.concatenate([
            jnp.concatenate([zeros_sub if i <= j else (q_sub[i] * jnp.exp(m[i] - m[j + 1])).astype(BF16)
                             for i in range(n_sub)], axis=0)
            for j in range(n_sub - 1)], axis=1)
        k_off = jnp.concatenate([
            jnp.concatenate([k_sub[i].astype(BF16) if i == j else zeros_sub for i in range(n_sub)], axis=0)
            for j in range(n_sub - 1)], axis=1)
        return q_off, k_off, q_d.astype(BF16), k_d.astype(BF16), k_dec, q_inter, vv, jnp.exp(m[n_sub])

    def stage_attn(sa, sb):
        q_off_a, k_off_a, q_d_a, k_d_a, k_dec_a, q_inter_a, vv_a, e_last_a = sa
        q_off_b, k_off_b, q_d_b, k_d_b, k_dec_b, q_inter_b, vv_b, e_last_b = sb
        a_off = _dot_nt(jnp.concatenate([q_off_a, q_off_b], axis=1), _block_diag2(k_off_a, k_off_b))
        a_diag = _dot_nt(jnp.concatenate([q_d_a, q_d_b], axis=1), _block_diag2(k_d_a, k_d_b))
        upd_a = _dot_tn(vv_a, k_dec_a)
        upd_b = _dot_tn(vv_b, k_dec_b)
        q_inter = jnp.concatenate([q_inter_a, q_inter_b], axis=1)
        return a_off, a_diag, upd_a, upd_b, q_inter, _block_diag2(vv_a, vv_b), e_last_a, e_last_b

    def stage_output(at, st_a, st_b):
        a_off, a_diag, upd_a, upd_b, q_inter, vv2, e_last_a, e_last_b = at
        attn = a_off + jnp.where(diag_mask, a_diag, 0.0)
        o = _dot_nt(q_inter, _block_diag2(st_a.astype(BF16), st_b.astype(BF16)))
        o = o + _dot(attn.astype(BF16), vv2)
        return o[:, 0:d], o[:, d:2 * d], st_a * e_last_a + upd_a, st_b * e_last_b + upd_b

    def stage_store(ck, hd, o):
        r0 = hdr + ck * c
        c_g = base + 3 * hw + hd * d
        o = o * lax.rsqrt(jnp.mean(o * o, axis=-1, keepdims=True) + EPS)
        yb = o * hnorm_ref[:, hd * d:(hd + 1) * d] * _silu(pr[r0:r0 + c, c_g:c_g + d])
        mix[ck * c:(ck + 1) * c, lw + hd * d:lw + (hd + 1) * d] = yb.astype(BF16)

    states = [state_t[hd] for hd in range(n_heads)]
    while fillers:
        fill()
    for c0 in range(0, n_chunks, HGRN_STAGE_CHUNKS):
        cks = range(c0, c0 + HGRN_STAGE_CHUNKS)
        gates_all = {ck: stage_gates(ck) for ck in cks}
        attn_all = {ck: [stage_attn(stage_scores(ck, hd, *gates_all[ck]), stage_scores(ck, hd + 1, *gates_all[ck]))
                         for hd in range(0, n_heads, 2)] for ck in cks}
        outs_all = {}
        for ck in cks:
            outs = []
            for pi, hd in enumerate(range(0, n_heads, 2)):
                o_a, o_b, states[hd], states[hd + 1] = stage_output(attn_all[ck][pi], states[hd], states[hd + 1])
                outs += [o_a, o_b]
            outs_all[ck] = outs
        for ck in cks:
            for hd in range(n_heads):
                stage_store(ck, hd, outs_all[ck][hd])
    for hd in range(n_heads):
        state_t[hd] = states[hd]

    o_ref[...] = hres[...] + _dot(mix[...], w_out_ref[...])
    pcur[0:hdr, 0:lw] = jnp.where(g % n_t == 0, 0.0, pcur[t_blk:t_blk + hdr, 0:lw])
    pcur[hdr:, :] = pnew[...]
    hres[...] = h_ref[...]


def _even_call(h, gain, w_in, conv_w, conv_b, wg, ba, bx, lam, lb_logits, hnorm, w_out, layer_j):
    bsz, s, d = h.shape
    lw = conv_b.shape[1]
    hw = hnorm.shape[1]
    t = MIX_BLOCK_T
    n_t = s // t
    n_blk = bsz * n_t
    in_map, out_map = _pipeline_maps(n_blk, n_t)
    return pl.pallas_call(
        functools.partial(_even_kernel, layer_j=layer_j, n_t=n_t),
        grid=(n_blk + 1,),
        in_specs=[
            pl.BlockSpec((None, t, d), in_map),
            _const_spec(gain.shape),
            _layer_spec(w_in, layer_j),
            _const_spec(conv_w.shape),
            _const_spec(conv_b.shape),
            _const_spec(wg.shape),
            _const_spec(ba.shape),
            _const_spec(bx.shape),
            _const_spec(lam.shape),
            _const_spec(lb_logits.shape),
            _const_spec(hnorm.shape),
            _layer_spec(w_out, layer_j),
        ],
        out_specs=pl.BlockSpec((None, t, d), out_map),
        out_shape=jax.ShapeDtypeStruct((bsz, s, d), F32),
        scratch_shapes=[
            pltpu.VMEM((t + SUBLANES, w_in.shape[-1]), F32),
            pltpu.VMEM((t, w_in.shape[-1]), F32),
            pltpu.VMEM((SUBLANES, lw), F32),
            pltpu.VMEM((hw // HGRN_HEAD_DIM, HGRN_HEAD_DIM, HGRN_HEAD_DIM), F32),
            pltpu.VMEM((t, lw + hw), BF16),
            pltpu.VMEM((t, d), F32),
        ],
        compiler_params=pltpu.CompilerParams(
            dimension_semantics=("arbitrary",), vmem_limit_bytes=VMEM_LIMIT_BYTES),
        name="mix_even",
    )(h, gain, w_in, conv_w, conv_b, wg, ba, bx, lam, lb_logits, hnorm, w_out)


def _odd_kernel(h_ref, gain_ref, w_in_ref, sgu_w_ref, sgu_b_ref, pool_w_ref, pool_scale_ref,
                w_out_ref, o_ref, pcur, pnew, mix, hres, *, n_t):
    t_blk = h_ref.shape[0]
    pw = pool_scale_ref.shape[1]
    sw = (w_in_ref.shape[1] - pw) // 2
    gdim = sw // SGU_GROUPS
    pdim = pw // len(POOL_WINDOWS)
    hdr = POOL_HALO
    g = pl.program_id(0)
    blk_r = jnp.maximum(g - 1, 0)

    @pl.when(g == 0)
    def _():
        pcur[...] = jnp.zeros(pcur.shape, F32)
        hres[...] = jnp.zeros(hres.shape, F32)

    xn = _rms_norm(h_ref[...], gain_ref[...]).astype(BF16)
    fillers = _projection_pieces(xn, w_in_ref, pnew)

    def fill():
        if fillers:
            fillers.pop(0)()

    fill()
    fill()
    fill()

    ri = lax.broadcasted_iota(jnp.int32, (SGU_CHUNK, SGU_CHUNK), 0)
    ci = lax.broadcasted_iota(jnp.int32, (SGU_CHUNK, SGU_CHUNK), 1)
    causal = ci <= ri
    for gr in range(SGU_GROUPS):
        ln = slice(gr * gdim, (gr + 1) * gdim)
        u = _gelu_tanh(pcur[hdr:, gr * gdim:(gr + 1) * gdim])
        vg = _gelu_tanh(pcur[hdr:, sw + gr * gdim:sw + (gr + 1) * gdim])
        mu = jnp.mean(vg, axis=-1, keepdims=True)
        cen = vg - mu
        var = jnp.mean(cen * cen, axis=-1, keepdims=True)
        vn = (cen * lax.rsqrt(var + EPS)).astype(BF16)
        wm = jnp.where(causal, sgu_w_ref[gr], 0.0).astype(BF16)
        bias = sgu_b_ref[gr]
        for n in range(t_blk // SGU_CHUNK):
            rows = slice(n * SGU_CHUNK, (n + 1) * SGU_CHUNK)
            sv = _dot(wm, vn[rows, :]) + bias
            mix[rows, ln] = (u[rows, :] * sv).astype(BF16)
        fill()

    pos = (blk_r % n_t) * t_blk + lax.broadcasted_iota(jnp.int32, (t_blk, 1), 0) + 1
    for gi, win in enumerate(POOL_WINDOWS):
        ln = slice(gi * pdim, (gi + 1) * pdim)
        ext = pcur[:, 2 * sw + gi * pdim:2 * sw + (gi + 1) * pdim]
        acc = ext
        sh = 1
        while sh < win:
            acc = acc + pltpu.roll(acc, sh, 0)
            sh *= 2
        inv_count = 1.0 / jnp.minimum(pos, win).astype(F32)
        pooled = acc[hdr:, :] * inv_count - ext[hdr:, :]
        yd = _dot(pooled.astype(BF16), pool_w_ref[gi]) * pool_scale_ref[:, ln]
        mix[:, sw + gi * pdim:sw + (gi + 1) * pdim] = yd.astype(BF16)
    while fillers:
        fill()

    o_ref[...] = hres[...] + _dot(mix[...], w_out_ref[...])
    pcur[0:hdr, 2 * sw:] = jnp.where(g % n_t == 0, 0.0, pcur[t_blk:t_blk + hdr, 2 * sw:])
    pcur[hdr:, :] = pnew[...]
    hres[...] = h_ref[...]


def _odd_call(h, gain, w_in, sgu_w, sgu_b, pool_w, pool_scale, w_out, layer_j):
    bsz, s, d = h.shape
    mw = w_out.shape[-2]
    t = MIX_BLOCK_T
    n_t = s // t
    n_blk = bsz * n_t
    in_map, out_map = _pipeline_maps(n_blk, n_t)
    return pl.pallas_call(
        functools.partial(_odd_kernel, n_t=n_t),
        grid=(n_blk + 1,),
        in_specs=[
            pl.BlockSpec((None, t, d), in_map),
            _const_spec(gain.shape),
            _layer_spec(w_in, layer_j),
            _const_spec(sgu_w.shape),
            _const_spec(sgu_b.shape),
            _layer_spec(pool_w, layer_j),
            _const_spec(pool_scale.shape),
            _layer_spec(w_out, layer_j),
        ],
        out_specs=pl.BlockSpec((None, t, d), out_map),
        out_shape=jax.ShapeDtypeStruct((bsz, s, d), F32),
        scratch_shapes=[
            pltpu.VMEM((t + POOL_HALO, w_in.shape[-1]), F32),
            pltpu.VMEM((t, w_in.shape[-1]), F32),
            pltpu.VMEM((t, mw), BF16),
            pltpu.VMEM((t, d), F32),
        ],
        compiler_params=pltpu.CompilerParams(
            dimension_semantics=("arbitrary",), vmem_limit_bytes=VMEM_LIMIT_BYTES),
        name="mix_odd",
    )(h, gain, w_in, sgu_w, sgu_b, pool_w, pool_scale, w_out)


def _gate_block_diag(wa, wx):
    n_heads, hd, _ = wa.shape
    per = LRU_GROUP // hd
    groups = n_heads // per
    out = jnp.zeros((groups, LRU_GROUP, 2 * LRU_GROUP), wa.dtype)
    for hh in range(n_heads):
        g, p = divmod(hh, per)
        out = out.at[g, p * hd:(p + 1) * hd, p * hd:(p + 1) * hd].set(wa[hh])
        out = out.at[g, p * hd:(p + 1) * hd, LRU_GROUP + p * hd:LRU_GROUP + (p + 1) * hd].set(wx[hh])
    return out


def kernel(x, norm_mix, norm_ffn, w_in_even, conv_w, conv_b, lru_wa, lru_ba, lru_wx, lru_bx, lru_lambda, hgrn_lb_logits, hgrn_norm, w_out_even, w_in_odd, sgu_w, sgu_b, pool_w, pool_scale, w_out_odd, w_ffn_in, w_ffn_out, norm_final):
    bsz, s, d = x.shape
    depth = norm_mix.shape[0]
    assert s % MIX_BLOCK_T == 0 and (bsz * s) % FFN_BLOCK_M == 0
    assert MIX_BLOCK_T % SGU_CHUNK == 0 and MIX_BLOCK_T % HGRN_CHUNK == 0
    row = lambda a: a.reshape(1, -1)
    w_in_even, w_out_even, w_in_odd, w_out_odd, pool_w, w_ffn_in, w_ffn_out = (
        w.astype(BF16) for w in (w_in_even, w_out_even, w_in_odd, w_out_odd, pool_w, w_ffn_in, w_ffn_out))
    h = x
    for layer in range(depth):
        j = layer // 2
        if layer % 2 == 0:
            wg = _gate_block_diag(lru_wa[j], lru_wx[j]).astype(BF16)
            h = _even_call(h, row(norm_mix[layer]), w_in_even, conv_w[j], row(conv_b[j]),
                           wg, row(lru_ba[j]), row(lru_bx[j]), row(lru_lambda[j]), hgrn_lb_logits,
                           row(hgrn_norm[j]), w_out_even, j)
        else:
            h = _odd_call(h, row(norm_mix[layer]), w_in_odd, sgu_w[j], sgu_b[j][..., None], pool_w,
                          row(pool_scale[j]), w_out_odd, j)
        h = _ffn_call(h.reshape(bsz * s, d), row(norm_ffn[layer]), w_ffn_in, w_ffn_out, layer, row(norm_final),
                      final_norm=(layer == depth - 1)).reshape(bsz, s, d)
    return h
```

```python
import functools

import jax
import jax.numpy as jnp
from jax import lax
from jax.experimental import pallas as pl
from jax.experimental.pallas import tpu as pltpu

F32 = jnp.float32
BF16 = jnp.bfloat16

EPS = 1e-6
LRU_C = 8.0
CONV_WIDTH = 4
LRU_HEADS = 8
LRU_GROUP = 256
HGRN_HEAD_DIM = 128
HGRN_CHUNK = 64
HGRN_SUB = 16
HGRN_STAGE_CHUNKS = 2
SGU_CHUNK = 128
SGU_GROUPS = 4
POOL_WINDOWS = (2, 4, 8, 16)
POOL_HALO = 16
SUBLANES = 8

MIX_BLOCK_T = 512
ODD_BLOCK_T = 1024
FFN_BLOCK_M = 1024
FFN_CHUNK = 1024
MXU_PIECE = 256
VMEM_LIMIT_BYTES = 56 * 1024 * 1024


def _dot(a, b):
    return jnp.dot(a, b, preferred_element_type=F32)


def _dot_nt(a, b):
    return lax.dot_general(a, b, (((1,), (1,)), ((), ())), preferred_element_type=F32)


def _dot_tn(a, b):
    return lax.dot_general(a, b, (((0,), (0,)), ((), ())), preferred_element_type=F32)


def _rms_norm(x, gain):
    ms = jnp.mean(x * x, axis=-1, keepdims=True)
    return x * lax.rsqrt(ms + EPS) * gain


def _sigmoid(x):
    return 0.5 * jnp.tanh(0.5 * x) + 0.5


def _sigmoid_small_tail(x):
    e = jnp.exp(-jnp.abs(x))
    big = 1.0 / (1.0 + e)
    return jnp.where(x >= 0, big, e * big)


def _silu(x):
    hx = 0.5 * x
    return hx * jnp.tanh(hx) + hx


def _gelu_tanh(x):
    c = 0.7978845608028654
    return 0.5 * x * (1.0 + jnp.tanh(c * (x + 0.044715 * (x * x * x))))


def _split2_bf16(x):
    hi = x.astype(BF16)
    lo = (x - hi.astype(F32)).astype(BF16)
    return hi, lo


def _const_spec(shape):
    nd = len(shape)
    return pl.BlockSpec(shape, lambda *_: (0,) * nd, pipeline_mode=pl.Buffered(1))


def _layer_spec(stacked, layer):
    tail = stacked.shape[1:]
    return pl.BlockSpec((None,) + tail, lambda *_: (layer,) + (0,) * len(tail), pipeline_mode=pl.Buffered(1))


def _ffn_kernel(h_ref, gain_ref, w1_ref, w2_ref, gfin_ref, o_ref, *, final_norm):
    h = h_ref[...]
    hn = _rms_norm(h, gain_ref[...]).astype(BF16)
    acc = h
    d_ff = w1_ref.shape[1]
    for c in range(d_ff // FFN_CHUNK):
        cols = slice(c * FFN_CHUNK, (c + 1) * FFN_CHUNK)
        hid = _dot(hn, w1_ref[:, cols])
        hid = jnp.square(jnp.maximum(hid, 0.0)).astype(BF16)
        acc = acc + _dot(hid, w2_ref[cols, :])
    if final_norm:
        acc = _rms_norm(acc, gfin_ref[...])
    o_ref[...] = acc


def _ffn_call(h2d, gain, w1_all, w2_all, layer, gfin, final_norm):
    m, d = h2d.shape
    return pl.pallas_call(
        functools.partial(_ffn_kernel, final_norm=final_norm),
        grid=(m // FFN_BLOCK_M,),
        in_specs=[
            pl.BlockSpec((FFN_BLOCK_M, d), lambda i: (i, 0)),
            _const_spec((1, d)),
            _layer_spec(w1_all, layer),
            _layer_spec(w2_all, layer),
            _const_spec((1, d)),
        ],
        out_specs=pl.BlockSpec((FFN_BLOCK_M, d), lambda i: (i, 0)),
        out_shape=jax.ShapeDtypeStruct((m, d), F32),
        compiler_params=pltpu.CompilerParams(
            dimension_semantics=("parallel",), vmem_limit_bytes=VMEM_LIMIT_BYTES),
        name="ffn_final" if final_norm else "ffn",
    )(h2d, gain, w1_all, w2_all, gfin)


def _pipeline_maps(n_blk, n_t):
    def in_map(g):
        blk = jnp.minimum(g, n_blk - 1)
        return (blk // n_t, blk % n_t, 0)

    def out_map(g):
        blk = jnp.maximum(g - 1, 0)
        return (blk // n_t, blk % n_t, 0)

    return in_map, out_map


def _projection_pieces(xn, w_in_ref, pnew):
    def make(c0):
        def piece():
            pnew[:, c0:c0 + MXU_PIECE] = _dot(xn, w_in_ref[:, c0:c0 + MXU_PIECE])
        return piece

    return [make(c0) for c0 in range(0, w_in_ref.shape[1], MXU_PIECE)]


def _lru_scan(a, b, carry):
    t, w = a.shape
    g = t // SUBLANES
    a3 = a.reshape(g, SUBLANES, w)
    b3 = b.reshape(g, SUBLANES, w)
    row = lax.broadcasted_iota(jnp.int32, (g, SUBLANES, w), 1)
    sh = 1
    while sh < SUBLANES:
        keep = row >= sh
        a_prev = jnp.where(keep, pltpu.roll(a3, sh, 1), 1.0)
        b_prev = jnp.where(keep, pltpu.roll(b3, sh, 1), 0.0)
        b3 = a3 * b_prev + b3
        a3 = a3 * a_prev
        sh *= 2

    def chain():
        c = carry
        outs = []
        for i in range(g):
            hblk = a3[i] * c + b3[i]
            outs.append(hblk)
            c = hblk[SUBLANES - 1:SUBLANES, :]
        return jnp.concatenate(outs, axis=0), c

    return chain


def _block_diag2(a, b):
    za = jnp.zeros(a.shape, a.dtype)
    zb = jnp.zeros(b.shape, b.dtype)
    return jnp.concatenate([jnp.concatenate([a, zb], axis=1), jnp.concatenate([za, b], axis=1)], axis=0)


def _even_kernel(h_ref, gain_ref, w_in_ref, conv_w_ref, conv_b_ref, wg_ref, ba_ref, bx_ref,
                 lam_ref, lb_logits_ref, hnorm_ref, w_out_ref, o_ref,
                 pcur, pnew, lru_carry, state_t, mix, hres, *, layer_j, n_t):
    t_blk = h_ref.shape[0]
    lw = conv_b_ref.shape[1]
    hw = hnorm_ref.shape[1]
    n_heads = hw // HGRN_HEAD_DIM
    hdr = SUBLANES
    g = pl.program_id(0)
    blk_r = jnp.maximum(g - 1, 0)

    @pl.when(g == 0)
    def _():
        pcur[...] = jnp.zeros(pcur.shape, F32)
        hres[...] = jnp.zeros(hres.shape, F32)

    @pl.when(blk_r % n_t == 0)
    def _():
        lru_carry[...] = jnp.zeros_like(lru_carry)
        state_t[...] = jnp.zeros_like(state_t)

    xn = _rms_norm(h_ref[...], gain_ref[...]).astype(BF16)
    fillers = _projection_pieces(xn, w_in_ref, pnew)

    def fill():
        if fillers:
            fillers.pop(0)()

    pr = pcur
    fill()
    fill()
    fill()

    neg_lam = -lam_ref[...]
    softplus = jnp.maximum(neg_lam, 0.0) + jnp.log1p(jnp.exp(-jnp.abs(neg_lam)))
    log_a_scale = -LRU_C * softplus
    for gi in range(lw // LRU_GROUP):
        ln = slice(gi * LRU_GROUP, (gi + 1) * LRU_GROUP)
        ua = conv_b_ref[:, ln] + conv_w_ref[CONV_WIDTH - 1:CONV_WIDTH, ln] * pr[hdr:, ln]
        for k in range(CONV_WIDTH - 1):
            r0 = hdr - (CONV_WIDTH - 1) + k
            ua = ua + conv_w_ref[k:k + 1, ln] * pr[r0:r0 + t_blk, ln]
        gates = _dot(ua.astype(BF16), wg_ref[gi])
        fill()
        r = _sigmoid(gates[:, :LRU_GROUP] + ba_ref[:, ln])
        i_gate = _sigmoid(gates[:, LRU_GROUP:] + bx_ref[:, ln])
        log_a = log_a_scale[:, ln] * r
        a = jnp.exp(log_a)
        th = jnp.tanh(log_a)
        y2 = -2.0 * th / (1.0 - th)
        beta = jnp.where(y2 > 0.0, y2 * lax.rsqrt(y2), 0.0)
        b_term = beta * (i_gate * ua)
        chain = _lru_scan(a, b_term, lru_carry[0:1, ln])
        fill()
        hl, carry = chain()
        lru_carry[0:1, ln] = carry
        fill()
        mix[:, ln] = (hl * _gelu_tanh(pr[hdr:, lw + gi * LRU_GROUP:lw + (gi + 1) * LRU_GROUP])).astype(BF16)
        if gi == 0:
            fill()

    while len(fillers) > 2:
        fill()

    logits = lb_logits_ref[...]
    n_lb = logits.shape[0]
    mx = logits[0:1, :]
    for i in range(1, n_lb):
        mx = jnp.maximum(mx, logits[i:i + 1, :])
    es = [jnp.exp(logits[i:i + 1, :] - mx) for i in range(n_lb)]
    den = es[0]
    for e in es[1:]:
        den = den + e
    ps = [e / den for e in es]
    cum = ps[0]
    for p in ps[1:layer_j + 1]:
        cum = cum + p
    lb = cum - ps[0]
    one_m_lb = 1.0 - lb

    c = HGRN_CHUNK
    d = HGRN_HEAD_DIM
    n_sub = c // HGRN_SUB
    n_chunks = t_blk // c
    ri = lax.broadcasted_iota(jnp.int32, (c, c), 0)
    ci = lax.broadcasted_iota(jnp.int32, (c, c), 1)
    tri = (ci <= ri).astype(BF16)
    rp = lax.broadcasted_iota(jnp.int32, (c, 2 * c), 0)
    cp = lax.broadcasted_iota(jnp.int32, (c, 2 * c), 1) % c
    diag_mask = (cp <= rp) & ((rp // HGRN_SUB) == (cp // HGRN_SUB))
    zeros_sub = jnp.zeros((HGRN_SUB, d), BF16)
    base = 2 * lw

    def stage_gates(ck):
        r0 = hdr + ck * c
        f_gate = lb + one_m_lb * _sigmoid_small_tail(pr[r0:r0 + c, base + hw:base + 2 * hw])
        hi, lo = _split2_bf16(jnp.log(f_gate))
        bc = _dot(tri, jnp.concatenate([hi, lo], axis=1))
        return 1.0 - f_gate, bc

    def stage_scores(ck, hd, k_all, bc):
        r0 = hdr + ck * c
        ln = slice(hd * d, (hd + 1) * d)
        c_q = base + hd * d
        bcum = bc[:, ln] + bc[:, hw + hd * d:hw + (hd + 1) * d]
        kk = k_all[:, ln]
        qf = _silu(pr[r0:r0 + c, c_q:c_q + d])
        vv = pr[r0:r0 + c, c_q + 2 * hw:c_q + 2 * hw + d].astype(BF16)
        m = [jnp.zeros((1, d), F32)] + [bcum[(i + 1) * HGRN_SUB - 1:(i + 1) * HGRN_SUB, :] for i in range(n_sub)]
        start = jnp.concatenate([jnp.broadcast_to(m[i], (HGRN_SUB, d)) for i in range(n_sub)], axis=0)
        dec = jnp.exp(bcum - start)
        q_d = qf * dec
        k_d = kk * (1.0 / dec)
        q_sub = [q_d[i * HGRN_SUB:(i + 1) * HGRN_SUB, :] for i in range(n_sub)]
        k_sub = [k_d[i * HGRN_SUB:(i + 1) * HGRN_SUB, :] * jnp.exp(m[i + 1] - m[i]) for i in range(n_sub)]
        q_inter = jnp.concatenate([(q_sub[i] * jnp.exp(m[i])).astype(BF16) for i in range(n_sub)], axis=0)
        k_dec = jnp.concatenate(
            [(k_sub[i] * jnp.exp(m[n_sub] - m[i + 1])).astype(BF16) for i in range(n_sub)], axis=0)
        q_off = jnp.concatenate([
            jnp.concatenate([zeros_sub if i <= j else (q_sub[i] * jnp.exp(m[i] - m[j + 1])).astype(BF16)
                             for i in range(n_sub)], axis=0)
            for j in range(n_sub - 1)], axis=1)
        k_off = jnp.concatenate([
            jnp.concatenate([k_sub[i].astype(BF16) if i == j else zeros_sub for i in range(n_sub)], axis=0)
            for j in range(n_sub - 1)], axis=1)
        return q_off, k_off, q_d.astype(BF16), k_d.astype(BF16), k_dec, q_inter, vv, jnp.exp(m[n_sub])

    def stage_attn(sa, sb):
        q_off_a, k_off_a, q_d_a, k_d_a, k_dec_a, q_inter_a, vv_a, e_last_a = sa
        q_off_b, k_off_b, q_d_b, k_d_b, k_dec_b, q_inter_b, vv_b, e_last_b = sb
        a_off = _dot_nt(jnp.concatenate([q_off_a, q_off_b], axis=1), _block_diag2(k_off_a, k_off_b))
        a_diag = _dot_nt(jnp.concatenate([q_d_a, q_d_b], axis=1), _block_diag2(k_d_a, k_d_b))
        upd_a = _dot_tn(vv_a, k_dec_a)
        upd_b = _dot_tn(vv_b, k_dec_b)
        q_inter = jnp.concatenate([q_inter_a, q_inter_b], axis=1)
        return a_off, a_diag, upd_a, upd_b, q_inter, _block_diag2(vv_a, vv_b), e_last_a, e_last_b

    def stage_output(at, st_a, st_b):
        a_off, a_diag, upd_a, upd_b, q_inter, vv2, e_last_a, e_last_b = at
        attn = a_off + jnp.where(diag_mask, a_diag, 0.0)
        o = _dot_nt(q_inter, _block_diag2(st_a.astype(BF16), st_b.astype(BF16)))
        o = o + _dot(attn.astype(BF16), vv2)
        return o[:, 0:d], o[:, d:2 * d], st_a * e_last_a + upd_a, st_b * e_last_b + upd_b

    def stage_store(ck, hd, o):
        r0 = hdr + ck * c
        c_g = base + 3 * hw + hd * d
        o = o * lax.rsqrt(jnp.mean(o * o, axis=-1, keepdims=True) + EPS)
        yb = o * hnorm_ref[:, hd * d:(hd + 1) * d] * _silu(pr[r0:r0 + c, c_g:c_g + d])
        mix[ck * c:(ck + 1) * c, lw + hd * d:lw + (hd + 1) * d] = yb.astype(BF16)

    states = [state_t[hd] for hd in range(n_heads)]
    while fillers:
        fill()
    for c0 in range(0, n_chunks, HGRN_STAGE_CHUNKS):
        cks = range(c0, c0 + HGRN_STAGE_CHUNKS)
        gates_all = {ck: stage_gates(ck) for ck in cks}
        attn_all = {ck: [stage_attn(stage_scores(ck, hd, *gates_all[ck]), stage_scores(ck, hd + 1, *gates_all[ck]))
                         for hd in range(0, n_heads, 2)] for ck in cks}
        outs_all = {}
        for ck in cks:
            outs = []
            for pi, hd in enumerate(range(0, n_heads, 2)):
                o_a, o_b, states[hd], states[hd + 1] = stage_output(attn_all[ck][pi], states[hd], states[hd + 1])
                outs += [o_a, o_b]
            outs_all[ck] = outs
        for ck in cks:
            for hd in range(n_heads):
                stage_store(ck, hd, outs_all[ck][hd])
    for hd in range(n_heads):
        state_t[hd] = states[hd]

    o_ref[...] = hres[...] + _dot(mix[...], w_out_ref[...])
    pcur[0:hdr, 0:lw] = jnp.where(g % n_t == 0, 0.0, pcur[t_blk:t_blk + hdr, 0:lw])
    pcur[hdr:, :] = pnew[...]
    hres[...] = h_ref[...]


def _even_call(h, gain, w_in, conv_w, conv_b, wg, ba, bx, lam, lb_logits, hnorm, w_out, layer_j):
    bsz, s, d = h.shape
    lw = conv_b.shape[1]
    hw = hnorm.shape[1]
    t = MIX_BLOCK_T
    n_t = s // t
    n_blk = bsz * n_t
    in_map, out_map = _pipeline_maps(n_blk, n_t)
    return pl.pallas_call(
        functools.partial(_even_kernel, layer_j=layer_j, n_t=n_t),
        grid=(n_blk + 1,),
        in_specs=[
            pl.BlockSpec((None, t, d), in_map),
            _const_spec(gain.shape),
            _layer_spec(w_in, layer_j),
            _const_spec(conv_w.shape),
            _const_spec(conv_b.shape),
            _const_spec(wg.shape),
            _const_spec(ba.shape),
            _const_spec(bx.shape),
            _const_spec(lam.shape),
            _const_spec(lb_logits.shape),
            _const_spec(hnorm.shape),
            _layer_spec(w_out, layer_j),
        ],
        out_specs=pl.BlockSpec((None, t, d), out_map),
        out_shape=jax.ShapeDtypeStruct((bsz, s, d), F32),
        scratch_shapes=[
            pltpu.VMEM((t + SUBLANES, w_in.shape[-1]), F32),
            pltpu.VMEM((t, w_in.shape[-1]), F32),
            pltpu.VMEM((SUBLANES, lw), F32),
            pltpu.VMEM((hw // HGRN_HEAD_DIM, HGRN_HEAD_DIM, HGRN_HEAD_DIM), F32),
            pltpu.VMEM((t, lw + hw), BF16),
            pltpu.VMEM((t, d), F32),
        ],
        compiler_params=pltpu.CompilerParams(
            dimension_semantics=("arbitrary",), vmem_limit_bytes=VMEM_LIMIT_BYTES),
        name="mix_even",
    )(h, gain, w_in, conv_w, conv_b, wg, ba, bx, lam, lb_logits, hnorm, w_out)


def _odd_kernel(h_ref, gain_ref, w_in_ref, sgu_w_ref, sgu_b_ref, pool_w_ref, pool_scale_ref,
                w_out_ref, o_ref, pcur, pnew, mix, hres, *, n_t):
    t_blk = h_ref.shape[0]
    pw = pool_scale_ref.shape[1]
    sw = (w_in_ref.shape[1] - pw) // 2
    gdim = sw // SGU_GROUPS
    pdim = pw // len(POOL_WINDOWS)
    hdr = POOL_HALO
    g = pl.program_id(0)
    blk_r = jnp.maximum(g - 1, 0)

    @pl.when(g == 0)
    def _():
        pcur[...] = jnp.zeros(pcur.shape, F32)
        hres[...] = jnp.zeros(hres.shape, F32)

    xn = _rms_norm(h_ref[...], gain_ref[...]).astype(BF16)
    fillers = _projection_pieces(xn, w_in_ref, pnew)

    def fill():
        if fillers:
            fillers.pop(0)()

    fill()
    fill()
    fill()

    ri = lax.broadcasted_iota(jnp.int32, (SGU_CHUNK, SGU_CHUNK), 0)
    ci = lax.broadcasted_iota(jnp.int32, (SGU_CHUNK, SGU_CHUNK), 1)
    causal = ci <= ri
    for gr in range(SGU_GROUPS):
        ln = slice(gr * gdim, (gr + 1) * gdim)
        u = _gelu_tanh(pcur[hdr:, gr * gdim:(gr + 1) * gdim])
        vg = _gelu_tanh(pcur[hdr:, sw + gr * gdim:sw + (gr + 1) * gdim])
        mu = jnp.mean(vg, axis=-1, keepdims=True)
        cen = vg - mu
        var = jnp.mean(cen * cen, axis=-1, keepdims=True)
        vn = (cen * lax.rsqrt(var + EPS)).astype(BF16)
        wm = jnp.where(causal, sgu_w_ref[gr], 0.0).astype(BF16)
        bias = sgu_b_ref[gr]
        for n in range(t_blk // SGU_CHUNK):
            rows = slice(n * SGU_CHUNK, (n + 1) * SGU_CHUNK)
            sv = _dot(wm, vn[rows, :]) + bias
            mix[rows, ln] = (u[rows, :] * sv).astype(BF16)
        fill()

    pos = (blk_r % n_t) * t_blk + lax.broadcasted_iota(jnp.int32, (t_blk, 1), 0) + 1
    for gi, win in enumerate(POOL_WINDOWS):
        ln = slice(gi * pdim, (gi + 1) * pdim)
        ext = pcur[:, 2 * sw + gi * pdim:2 * sw + (gi + 1) * pdim]
        acc = ext
        sh = 1
        while sh < win:
            acc = acc + pltpu.roll(acc, sh, 0)
            sh *= 2
        inv_count = 1.0 / jnp.minimum(pos, win).astype(F32)
        pooled = acc[hdr:, :] * inv_count - ext[hdr:, :]
        yd = _dot(pooled.astype(BF16), pool_w_ref[gi]) * pool_scale_ref[:, ln]
        mix[:, sw + gi * pdim:sw + (gi + 1) * pdim] = yd.astype(BF16)
    while fillers:
        fill()

    o_ref[...] = hres[...] + _dot(mix[...], w_out_ref[...])
    pcur[0:hdr, 2 * sw:] = jnp.where(g % n_t == 0, 0.0, pcur[t_blk:t_blk + hdr, 2 * sw:])
    pcur[hdr:, :] = pnew[...]
    hres[...] = h_ref[...]


def _odd_call(h, gain, w_in, sgu_w, sgu_b, pool_w, pool_scale, w_out, layer_j):
    bsz, s, d = h.shape
    mw = w_out.shape[-2]
    t = ODD_BLOCK_T
    n_t = s // t
    n_blk = bsz * n_t
    in_map, out_map = _pipeline_maps(n_blk, n_t)
    return pl.pallas_call(
        functools.partial(_odd_kernel, n_t=n_t),
        grid=(n_blk + 1,),
        in_specs=[
            pl.BlockSpec((None, t, d), in_map),
            _const_spec(gain.shape),
            _layer_spec(w_in, layer_j),
            _const_spec(sgu_w.shape),
            _const_spec(sgu_b.shape),
            _layer_spec(pool_w, layer_j),
            _const_spec(pool_scale.shape),
            _layer_spec(w_out, layer_j),
        ],
        out_specs=pl.BlockSpec((None, t, d), out_map),
        out_shape=jax.ShapeDtypeStruct((bsz, s, d), F32),
        scratch_shapes=[
            pltpu.VMEM((t + POOL_HALO, w_in.shape[-1]), F32),
            pltpu.VMEM((t, w_in.shape[-1]), F32),
            pltpu.VMEM((t, mw), BF16),
            pltpu.VMEM((t, d), F32),
        ],
        compiler_params=pltpu.CompilerParams(
            dimension_semantics=("arbitrary",), vmem_limit_bytes=VMEM_LIMIT_BYTES),
        name="mix_odd",
    )(h, gain, w_in, sgu_w, sgu_b, pool_w, pool_scale, w_out)


def _gate_block_diag(wa, wx):
    n_heads, hd, _ = wa.shape
    per = LRU_GROUP // hd
    groups = n_heads // per
    out = jnp.zeros((groups, LRU_GROUP, 2 * LRU_GROUP), wa.dtype)
    for hh in range(n_heads):
        g, p = divmod(hh, per)
        out = out.at[g, p * hd:(p + 1) * hd, p * hd:(p + 1) * hd].set(wa[hh])
        out = out.at[g, p * hd:(p + 1) * hd, LRU_GROUP + p * hd:LRU_GROUP + (p + 1) * hd].set(wx[hh])
    return out


def kernel(x, norm_mix, norm_ffn, w_in_even, conv_w, conv_b, lru_wa, lru_ba, lru_wx, lru_bx, lru_lambda, hgrn_lb_logits, hgrn_norm, w_out_even, w_in_odd, sgu_w, sgu_b, pool_w, pool_scale, w_out_odd, w_ffn_in, w_ffn_out, norm_final):
    bsz, s, d = x.shape
    depth = norm_mix.shape[0]
    assert s % MIX_BLOCK_T == 0 and s % ODD_BLOCK_T == 0 and (bsz * s) % FFN_BLOCK_M == 0
    assert ODD_BLOCK_T % SGU_CHUNK == 0 and MIX_BLOCK_T % HGRN_CHUNK == 0
    row = lambda a: a.reshape(1, -1)
    w_in_even, w_out_even, w_in_odd, w_out_odd, pool_w, w_ffn_in, w_ffn_out = (
        w.astype(BF16) for w in (w_in_even, w_out_even, w_in_odd, w_out_odd, pool_w, w_ffn_in, w_ffn_out))
    h = x
    for layer in range(depth):
        j = layer // 2
        if layer % 2 == 0:
            wg = _gate_block_diag(lru_wa[j], lru_wx[j]).astype(BF16)
            h = _even_call(h, row(norm_mix[layer]), w_in_even, conv_w[j], row(conv_b[j]),
                           wg, row(lru_ba[j]), row(lru_bx[j]), row(lru_lambda[j]), hgrn_lb_logits,
                           row(hgrn_norm[j]), w_out_even, j)
        else:
            h = _odd_call(h, row(norm_mix[layer]), w_in_odd, sgu_w[j], sgu_b[j][..., None], pool_w,
                          row(pool_scale[j]), w_out_odd, j)
        h = _ffn_call(h.reshape(bsz * s, d), row(norm_ffn[layer]), w_ffn_in, w_ffn_out, layer, row(norm_final),
                      final_norm=(layer == depth - 1)).reshape(bsz, s, d)
    return h
```

```python
import functools

import jax
import jax.numpy as jnp
from jax import lax
from jax.experimental import pallas as pl
from jax.experimental.pallas import tpu as pltpu

F32 = jnp.float32
BF16 = jnp.bfloat16

EPS = 1e-6
LRU_C = 8.0
CONV_WIDTH = 4
LRU_HEADS = 8
LRU_GROUP = 256
HGRN_HEAD_DIM = 128
HGRN_CHUNK = 64
HGRN_SUB = 16
HGRN_STAGE_CHUNKS = 2
SGU_CHUNK = 128
SGU_GROUPS = 4
POOL_WINDOWS = (2, 4, 8, 16)
POOL_HALO = 16
SUBLANES = 8

MIX_BLOCK_T = 512
ODD_BLOCK_T = 1024
FFN_BLOCK_M = 2048
FFN_SLAB = 256
FFN_CHUNK = 1024
MXU_PIECE = 256
VMEM_LIMIT_BYTES = 56 * 1024 * 1024


def _dot(a, b):
    return jnp.dot(a, b, preferred_element_type=F32)


def _dot_nt(a, b):
    return lax.dot_general(a, b, (((1,), (1,)), ((), ())), preferred_element_type=F32)


def _dot_tn(a, b):
    return lax.dot_general(a, b, (((0,), (0,)), ((), ())), preferred_element_type=F32)


def _rms_norm(x, gain):
    ms = jnp.mean(x * x, axis=-1, keepdims=True)
    return x * lax.rsqrt(ms + EPS) * gain


def _sigmoid(x):
    return 0.5 * jnp.tanh(0.5 * x) + 0.5


def _sigmoid_small_tail(x):
    e = jnp.exp(-jnp.abs(x))
    big = 1.0 / (1.0 + e)
    return jnp.where(x >= 0, big, e * big)


def _silu(x):
    hx = 0.5 * x
    return hx * jnp.tanh(hx) + hx


def _gelu_tanh(x):
    c = 0.7978845608028654
    return 0.5 * x * (1.0 + jnp.tanh(c * (x + 0.044715 * (x * x * x))))


def _split2_bf16(x):
    hi = x.astype(BF16)
    lo = (x - hi.astype(F32)).astype(BF16)
    return hi, lo


def _const_spec(shape):
    nd = len(shape)
    return pl.BlockSpec(shape, lambda *_: (0,) * nd, pipeline_mode=pl.Buffered(1))


def _layer_spec(stacked, layer):
    tail = stacked.shape[1:]
    return pl.BlockSpec((None,) + tail, lambda *_: (layer,) + (0,) * len(tail), pipeline_mode=pl.Buffered(1))


def _ffn_kernel(h_ref, gain_ref, w1_ref, w2_ref, gfin_ref, o_ref, *, final_norm):
    d_ff = w1_ref.shape[1]
    for r0 in range(0, h_ref.shape[0], FFN_SLAB):
        rows = slice(r0, r0 + FFN_SLAB)
        h = h_ref[rows, :]
        hn = _rms_norm(h, gain_ref[...]).astype(BF16)
        acc = h
        for c in range(d_ff // FFN_CHUNK):
            cols = slice(c * FFN_CHUNK, (c + 1) * FFN_CHUNK)
            hid = _dot(hn, w1_ref[:, cols])
            hid = jnp.square(jnp.maximum(hid, 0.0)).astype(BF16)
            acc = acc + _dot(hid, w2_ref[cols, :])
        if final_norm:
            acc = _rms_norm(acc, gfin_ref[...])
        o_ref[rows, :] = acc


def _ffn_call(h2d, gain, w1_all, w2_all, layer, gfin, final_norm):
    m, d = h2d.shape
    return pl.pallas_call(
        functools.partial(_ffn_kernel, final_norm=final_norm),
        grid=(m // FFN_BLOCK_M,),
        in_specs=[
            pl.BlockSpec((FFN_BLOCK_M, d), lambda i: (i, 0)),
            _const_spec((1, d)),
            _layer_spec(w1_all, layer),
            _layer_spec(w2_all, layer),
            _const_spec((1, d)),
        ],
        out_specs=pl.BlockSpec((FFN_BLOCK_M, d), lambda i: (i, 0)),
        out_shape=jax.ShapeDtypeStruct((m, d), F32),
        compiler_params=pltpu.CompilerParams(
            dimension_semantics=("parallel",), vmem_limit_bytes=VMEM_LIMIT_BYTES),
        name="ffn_final" if final_norm else "ffn",
    )(h2d, gain, w1_all, w2_all, gfin)


def _pipeline_maps(n_blk, n_t):
    def in_map(g):
        blk = jnp.minimum(g, n_blk - 1)
        return (blk // n_t, blk % n_t, 0)

    def out_map(g):
        blk = jnp.maximum(g - 1, 0)
        return (blk // n_t, blk % n_t, 0)

    return in_map, out_map


def _projection_pieces(xn, w_in_ref, pnew):
    def make(c0):
        def piece():
            pnew[:, c0:c0 + MXU_PIECE] = _dot(xn, w_in_ref[:, c0:c0 + MXU_PIECE])
        return piece

    return [make(c0) for c0 in range(0, w_in_ref.shape[1], MXU_PIECE)]


def _lru_scan(a, b, carry):
    t, w = a.shape
    g = t // SUBLANES
    a3 = a.reshape(g, SUBLANES, w)
    b3 = b.reshape(g, SUBLANES, w)
    row = lax.broadcasted_iota(jnp.int32, (g, SUBLANES, w), 1)
    sh = 1
    while sh < SUBLANES:
        keep = row >= sh
        a_prev = jnp.where(keep, pltpu.roll(a3, sh, 1), 1.0)
        b_prev = jnp.where(keep, pltpu.roll(b3, sh, 1), 0.0)
        b3 = a3 * b_prev + b3
        a3 = a3 * a_prev
        sh *= 2

    def chain():
        c = carry
        outs = []
        for i in range(g):
            hblk = a3[i] * c + b3[i]
            outs.append(hblk)
            c = hblk[SUBLANES - 1:SUBLANES, :]
        return jnp.concatenate(outs, axis=0), c

    return chain


def _block_diag2(a, b):
    za = jnp.zeros(a.shape, a.dtype)
    zb = jnp.zeros(b.shape, b.dtype)
    return jnp.concatenate([jnp.concatenate([a, zb], axis=1), jnp.concatenate([za, b], axis=1)], axis=0)


def _even_kernel(h_ref, gain_ref, w_in_ref, conv_w_ref, conv_b_ref, wg_ref, ba_ref, bx_ref,
                 lam_ref, lb_logits_ref, hnorm_ref, w_out_ref, o_ref,
                 pcur, pnew, lru_carry, state_t, mix, hres, *, layer_j, n_t):
    t_blk = h_ref.shape[0]
    lw = conv_b_ref.shape[1]
    hw = hnorm_ref.shape[1]
    n_heads = hw // HGRN_HEAD_DIM
    hdr = SUBLANES
    g = pl.program_id(0)
    blk_r = jnp.maximum(g - 1, 0)

    @pl.when(g == 0)
    def _():
        pcur[...] = jnp.zeros(pcur.shape, F32)
        hres[...] = jnp.zeros(hres.shape, F32)

    @pl.when(blk_r % n_t == 0)
    def _():
        lru_carry[...] = jnp.zeros_like(lru_carry)
        state_t[...] = jnp.zeros_like(state_t)

    xn = _rms_norm(h_ref[...], gain_ref[...]).astype(BF16)
    fillers = _projection_pieces(xn, w_in_ref, pnew)

    def fill():
        if fillers:
            fillers.pop(0)()

    pr = pcur
    fill()
    fill()
    fill()

    neg_lam = -lam_ref[...]
    softplus = jnp.maximum(neg_lam, 0.0) + jnp.log1p(jnp.exp(-jnp.abs(neg_lam)))
    log_a_scale = -LRU_C * softplus
    for gi in range(lw // LRU_GROUP):
        ln = slice(gi * LRU_GROUP, (gi + 1) * LRU_GROUP)
        ua = conv_b_ref[:, ln] + conv_w_ref[CONV_WIDTH - 1:CONV_WIDTH, ln] * pr[hdr:, ln]
        for k in range(CONV_WIDTH - 1):
            r0 = hdr - (CONV_WIDTH - 1) + k
            ua = ua + conv_w_ref[k:k + 1, ln] * pr[r0:r0 + t_blk, ln]
        gates = _dot(ua.astype(BF16), wg_ref[gi])
        fill()
        r = _sigmoid(gates[:, :LRU_GROUP] + ba_ref[:, ln])
        i_gate = _sigmoid(gates[:, LRU_GROUP:] + bx_ref[:, ln])
        log_a = log_a_scale[:, ln] * r
        a = jnp.exp(log_a)
        th = jnp.tanh(log_a)
        y2 = -2.0 * th / (1.0 - th)
        beta = jnp.where(y2 > 0.0, y2 * lax.rsqrt(y2), 0.0)
        b_term = beta * (i_gate * ua)
        chain = _lru_scan(a, b_term, lru_carry[0:1, ln])
        fill()
        hl, carry = chain()
        lru_carry[0:1, ln] = carry
        fill()
        mix[:, ln] = (hl * _gelu_tanh(pr[hdr:, lw + gi * LRU_GROUP:lw + (gi + 1) * LRU_GROUP])).astype(BF16)
        if gi == 0:
            fill()

    while len(fillers) > 2:
        fill()

    logits = lb_logits_ref[...]
    n_lb = logits.shape[0]
    mx = logits[0:1, :]
    for i in range(1, n_lb):
        mx = jnp.maximum(mx, logits[i:i + 1, :])
    es = [jnp.exp(logits[i:i + 1, :] - mx) for i in range(n_lb)]
    den = es[0]
    for e in es[1:]:
        den = den + e
    ps = [e / den for e in es]
    cum = ps[0]
    for p in ps[1:layer_j + 1]:
        cum = cum + p
    lb = cum - ps[0]
    one_m_lb = 1.0 - lb

    c = HGRN_CHUNK
    d = HGRN_HEAD_DIM
    n_sub = c // HGRN_SUB
    n_chunks = t_blk // c
    ri = lax.broadcasted_iota(jnp.int32, (c, c), 0)
    ci = lax.broadcasted_iota(jnp.int32, (c, c), 1)
    tri = (ci <= ri).astype(BF16)
    rp = lax.broadcasted_iota(jnp.int32, (c, 2 * c), 0)
    cp = lax.broadcasted_iota(jnp.int32, (c, 2 * c), 1) % c
    diag_mask = (cp <= rp) & ((rp // HGRN_SUB) == (cp // HGRN_SUB))
    zeros_sub = jnp.zeros((HGRN_SUB, d), BF16)
    base = 2 * lw

    def stage_gates(ck):
        r0 = hdr + ck * c
        f_gate = lb + one_m_lb * _sigmoid_small_tail(pr[r0:r0 + c, base + hw:base + 2 * hw])
        hi, lo = _split2_bf16(jnp.log(f_gate))
        bc = _dot(tri, jnp.concatenate([hi, lo], axis=1))
        return 1.0 - f_gate, bc

    def stage_scores(ck, hd, k_all, bc):
        r0 = hdr + ck * c
        ln = slice(hd * d, (hd + 1) * d)
        c_q = base + hd * d
        bcum = bc[:, ln] + bc[:, hw + hd * d:hw + (hd + 1) * d]
        kk = k_all[:, ln]
        qf = _silu(pr[r0:r0 + c, c_q:c_q + d])
        vv = pr[r0:r0 + c, c_q + 2 * hw:c_q + 2 * hw + d].astype(BF16)
        m = [jnp.zeros((1, d), F32)] + [bcum[(i + 1) * HGRN_SUB - 1:(i + 1) * HGRN_SUB, :] for i in range(n_sub)]
        start = jnp.concatenate([jnp.broadcast_to(m[i], (HGRN_SUB, d)) for i in range(n_sub)], axis=0)
        dec = jnp.exp(bcum - start)
        q_d = qf * dec
        k_d = kk * (1.0 / dec)
        q_sub = [q_d[i * HGRN_SUB:(i + 1) * HGRN_SUB, :] for i in range(n_sub)]
        k_sub = [k_d[i * HGRN_SUB:(i + 1) * HGRN_SUB, :] * jnp.exp(m[i + 1] - m[i]) for i in range(n_sub)]
        q_inter = jnp.concatenate([(q_sub[i] * jnp.exp(m[i])).astype(BF16) for i in range(n_sub)], axis=0)
        k_dec = jnp.concatenate(
            [(k_sub[i] * jnp.exp(m[n_sub] - m[i + 1])).astype(BF16) for i in range(n_sub)], axis=0)
        q_off = jnp.concatenate([
            jnp.concatenate([zeros_sub if i <= j else (q_sub[i] * jnp.exp(m[i] - m[j + 1])).astype(BF16)
                             for i in range(n_sub)], axis=0)
            for j in range(n_sub - 1)], axis=1)
        k_off = jnp.concatenate([
            jnp.concatenate([k_sub[i].astype(BF16) if i == j else zeros_sub for i in range(n_sub)], axis=0)
            for j in range(n_sub - 1)], axis=1)
        return q_off, k_off, q_d.astype(BF16), k_d.astype(BF16), k_dec, q_inter, vv, jnp.exp(m[n_sub])

    def stage_attn(sa, sb):
        q_off_a, k_off_a, q_d_a, k_d_a, k_dec_a, q_inter_a, vv_a, e_last_a = sa
        q_off_b, k_off_b, q_d_b, k_d_b, k_dec_b, q_inter_b, vv_b, e_last_b = sb
        a_off = _dot_nt(jnp.concatenate([q_off_a, q_off_b], axis=1), _block_diag2(k_off_a, k_off_b))
        a_diag = _dot_nt(jnp.concatenate([q_d_a, q_d_b], axis=1), _block_diag2(k_d_a, k_d_b))
        upd_a = _dot_tn(vv_a, k_dec_a)
        upd_b = _dot_tn(vv_b, k_dec_b)
        q_inter = jnp.concatenate([q_inter_a, q_inter_b], axis=1)
        return a_off, a_diag, upd_a, upd_b, q_inter, _block_diag2(vv_a, vv_b), e_last_a, e_last_b

    def stage_output(at, st_a, st_b):
        a_off, a_diag, upd_a, upd_b, q_inter, vv2, e_last_a, e_last_b = at
        attn = a_off + jnp.where(diag_mask, a_diag, 0.0)
        o = _dot_nt(q_inter, _block_diag2(st_a.astype(BF16), st_b.astype(BF16)))
        o = o + _dot(attn.astype(BF16), vv2)
        return o[:, 0:d], o[:, d:2 * d], st_a * e_last_a + upd_a, st_b * e_last_b + upd_b

    def stage_store(ck, hd, o):
        r0 = hdr + ck * c
        c_g = base + 3 * hw + hd * d
        o = o * lax.rsqrt(jnp.mean(o * o, axis=-1, keepdims=True) + EPS)
        yb = o * hnorm_ref[:, hd * d:(hd + 1) * d] * _silu(pr[r0:r0 + c, c_g:c_g + d])
        mix[ck * c:(ck + 1) * c, lw + hd * d:lw + (hd + 1) * d] = yb.astype(BF16)

    states = [state_t[hd] for hd in range(n_heads)]
    while fillers:
        fill()
    for c0 in range(0, n_chunks, HGRN_STAGE_CHUNKS):
        cks = range(c0, c0 + HGRN_STAGE_CHUNKS)
        gates_all = {ck: stage_gates(ck) for ck in cks}
        attn_all = {ck: [stage_attn(stage_scores(ck, hd, *gates_all[ck]), stage_scores(ck, hd + 1, *gates_all[ck]))
                         for hd in range(0, n_heads, 2)] for ck in cks}
        outs_all = {}
        for ck in cks:
            outs = []
            for pi, hd in enumerate(range(0, n_heads, 2)):
                o_a, o_b, states[hd], states[hd + 1] = stage_output(attn_all[ck][pi], states[hd], states[hd + 1])
                outs += [o_a, o_b]
            outs_all[ck] = outs
        for ck in cks:
            for hd in range(n_heads):
                stage_store(ck, hd, outs_all[ck][hd])
    for hd in range(n_heads):
        state_t[hd] = states[hd]

    o_ref[...] = hres[...] + _dot(mix[...], w_out_ref[...])
    pcur[0:hdr, 0:lw] = jnp.where(g % n_t == 0, 0.0, pcur[t_blk:t_blk + hdr, 0:lw])
    pcur[hdr:, :] = pnew[...]
    hres[...] = h_ref[...]


def _even_call(h, gain, w_in, conv_w, conv_b, wg, ba, bx, lam, lb_logits, hnorm, w_out, layer_j):
    bsz, s, d = h.shape
    lw = conv_b.shape[1]
    hw = hnorm.shape[1]
    t = MIX_BLOCK_T
    n_t = s // t
    n_blk = bsz * n_t
    in_map, out_map = _pipeline_maps(n_blk, n_t)
    return pl.pallas_call(
        functools.partial(_even_kernel, layer_j=layer_j, n_t=n_t),
        grid=(n_blk + 1,),
        in_specs=[
            pl.BlockSpec((None, t, d), in_map),
            _const_spec(gain.shape),
            _layer_spec(w_in, layer_j),
            _const_spec(conv_w.shape),
            _const_spec(conv_b.shape),
            _const_spec(wg.shape),
            _const_spec(ba.shape),
            _const_spec(bx.shape),
            _const_spec(lam.shape),
            _const_spec(lb_logits.shape),
            _const_spec(hnorm.shape),
            _layer_spec(w_out, layer_j),
        ],
        out_specs=pl.BlockSpec((None, t, d), out_map),
        out_shape=jax.ShapeDtypeStruct((bsz, s, d), F32),
        scratch_shapes=[
            pltpu.VMEM((t + SUBLANES, w_in.shape[-1]), F32),
            pltpu.VMEM((t, w_in.shape[-1]), F32),
            pltpu.VMEM((SUBLANES, lw), F32),
            pltpu.VMEM((hw // HGRN_HEAD_DIM, HGRN_HEAD_DIM, HGRN_HEAD_DIM), F32),
            pltpu.VMEM((t, lw + hw), BF16),
            pltpu.VMEM((t, d), F32),
        ],
        compiler_params=pltpu.CompilerParams(
            dimension_semantics=("arbitrary",), vmem_limit_bytes=VMEM_LIMIT_BYTES),
        name="mix_even",
    )(h, gain, w_in, conv_w, conv_b, wg, ba, bx, lam, lb_logits, hnorm, w_out)


def _odd_kernel(h_ref, gain_ref, w_in_ref, sgu_w_ref, sgu_b_ref, pool_w_ref, pool_scale_ref,
                w_out_ref, o_ref, pcur, pnew, mix, hres, *, n_t):
    t_blk = h_ref.shape[0]
    pw = pool_scale_ref.shape[1]
    sw = (w_in_ref.shape[1] - pw) // 2
    gdim = sw // SGU_GROUPS
    pdim = pw // len(POOL_WINDOWS)
    hdr = POOL_HALO
    g = pl.program_id(0)
    blk_r = jnp.maximum(g - 1, 0)

    @pl.when(g == 0)
    def _():
        pcur[...] = jnp.zeros(pcur.shape, F32)
        hres[...] = jnp.zeros(hres.shape, F32)

    xn = _rms_norm(h_ref[...], gain_ref[...]).astype(BF16)
    fillers = _projection_pieces(xn, w_in_ref, pnew)

    def fill():
        if fillers:
            fillers.pop(0)()

    fill()
    fill()
    fill()

    ri = lax.broadcasted_iota(jnp.int32, (SGU_CHUNK, SGU_CHUNK), 0)
    ci = lax.broadcasted_iota(jnp.int32, (SGU_CHUNK, SGU_CHUNK), 1)
    causal = ci <= ri
    for gr in range(SGU_GROUPS):
        ln = slice(gr * gdim, (gr + 1) * gdim)
        u = _gelu_tanh(pcur[hdr:, gr * gdim:(gr + 1) * gdim])
        vg = _gelu_tanh(pcur[hdr:, sw + gr * gdim:sw + (gr + 1) * gdim])
        mu = jnp.mean(vg, axis=-1, keepdims=True)
        cen = vg - mu
        var = jnp.mean(cen * cen, axis=-1, keepdims=True)
        vn = (cen * lax.rsqrt(var + EPS)).astype(BF16)
        wm = jnp.where(causal, sgu_w_ref[gr], 0.0).astype(BF16)
        bias = sgu_b_ref[gr]
        for n in range(t_blk // SGU_CHUNK):
            rows = slice(n * SGU_CHUNK, (n + 1) * SGU_CHUNK)
            sv = _dot(wm, vn[rows, :]) + bias
            mix[rows, ln] = (u[rows, :] * sv).astype(BF16)
        fill()

    pos = (blk_r % n_t) * t_blk + lax.broadcasted_iota(jnp.int32, (t_blk, 1), 0) + 1
    for gi, win in enumerate(POOL_WINDOWS):
        ln = slice(gi * pdim, (gi + 1) * pdim)
        ext = pcur[:, 2 * sw + gi * pdim:2 * sw + (gi + 1) * pdim]
        acc = ext
        sh = 1
        while sh < win:
            acc = acc + pltpu.roll(acc, sh, 0)
            sh *= 2
        inv_count = 1.0 / jnp.minimum(pos, win).astype(F32)
        pooled = acc[hdr:, :] * inv_count - ext[hdr:, :]
        yd = _dot(pooled.astype(BF16), pool_w_ref[gi]) * pool_scale_ref[:, ln]
        mix[:, sw + gi * pdim:sw + (gi + 1) * pdim] = yd.astype(BF16)
    while fillers:
        fill()

    o_ref[...] = hres[...] + _dot(mix[...], w_out_ref[...])
    pcur[0:hdr, 2 * sw:] = jnp.where(g % n_t == 0, 0.0, pcur[t_blk:t_blk + hdr, 2 * sw:])
    pcur[hdr:, :] = pnew[...]
    hres[...] = h_ref[...]


def _odd_call(h, gain, w_in, sgu_w, sgu_b, pool_w, pool_scale, w_out, layer_j):
    bsz, s, d = h.shape
    mw = w_out.shape[-2]
    t = ODD_BLOCK_T
    n_t = s // t
    n_blk = bsz * n_t
    in_map, out_map = _pipeline_maps(n_blk, n_t)
    return pl.pallas_call(
        functools.partial(_odd_kernel, n_t=n_t),
        grid=(n_blk + 1,),
        in_specs=[
            pl.BlockSpec((None, t, d), in_map),
            _const_spec(gain.shape),
            _layer_spec(w_in, layer_j),
            _const_spec(sgu_w.shape),
            _const_spec(sgu_b.shape),
            _layer_spec(pool_w, layer_j),
            _const_spec(pool_scale.shape),
            _layer_spec(w_out, layer_j),
        ],
        out_specs=pl.BlockSpec((None, t, d), out_map),
        out_shape=jax.ShapeDtypeStruct((bsz, s, d), F32),
        scratch_shapes=[
            pltpu.VMEM((t + POOL_HALO, w_in.shape[-1]), F32),
            pltpu.VMEM((t, w_in.shape[-1]), F32),
            pltpu.VMEM((t, mw), BF16),
            pltpu.VMEM((t, d), F32),
        ],
        compiler_params=pltpu.CompilerParams(
            dimension_semantics=("arbitrary",), vmem_limit_bytes=VMEM_LIMIT_BYTES),
        name="mix_odd",
    )(h, gain, w_in, sgu_w, sgu_b, pool_w, pool_scale, w_out)


def _gate_block_diag(wa, wx):
    n_heads, hd, _ = wa.shape
    per = LRU_GROUP // hd
    groups = n_heads // per
    out = jnp.zeros((groups, LRU_GROUP, 2 * LRU_GROUP), wa.dtype)
    for hh in range(n_heads):
        g, p = divmod(hh, per)
        out = out.at[g, p * hd:(p + 1) * hd, p * hd:(p + 1) * hd].set(wa[hh])
        out = out.at[g, p * hd:(p + 1) * hd, LRU_GROUP + p * hd:LRU_GROUP + (p + 1) * hd].set(wx[hh])
    return out


def kernel(x, norm_mix, norm_ffn, w_in_even, conv_w, conv_b, lru_wa, lru_ba, lru_wx, lru_bx, lru_lambda, hgrn_lb_logits, hgrn_norm, w_out_even, w_in_odd, sgu_w, sgu_b, pool_w, pool_scale, w_out_odd, w_ffn_in, w_ffn_out, norm_final):
    bsz, s, d = x.shape
    depth = norm_mix.shape[0]
    assert s % MIX_BLOCK_T == 0 and s % ODD_BLOCK_T == 0 and (bsz * s) % FFN_BLOCK_M == 0
    assert ODD_BLOCK_T % SGU_CHUNK == 0 and MIX_BLOCK_T % HGRN_CHUNK == 0
    row = lambda a: a.reshape(1, -1)
    w_in_even, w_out_even, w_in_odd, w_out_odd, pool_w, w_ffn_in, w_ffn_out = (
        w.astype(BF16) for w in (w_in_even, w_out_even, w_in_odd, w_out_odd, pool_w, w_ffn_in, w_ffn_out))
    h = x
    for layer in range(depth):
        j = layer // 2
        if layer % 2 == 0:
            wg = _gate_block_diag(lru_wa[j], lru_wx[j]).astype(BF16)
            h = _even_call(h, row(norm_mix[layer]), w_in_even, conv_w[j], row(conv_b[j]),
                           wg, row(lru_ba[j]), row(lru_bx[j]), row(lru_lambda[j]), hgrn_lb_logits,
                           row(hgrn_norm[j]), w_out_even, j)
        else:
            h = _odd_call(h, row(norm_mix[layer]), w_in_odd, sgu_w[j], sgu_b[j][..., None], pool_w,
                          row(pool_scale[j]), w_out_odd, j)
        h = _ffn_call(h.reshape(bsz * s, d), row(norm_ffn[layer]), w_ffn_in, w_ffn_out, layer, row(norm_final),
                      final_norm=(layer == depth - 1)).reshape(bsz, s, d)
    return h
```

```python
import functools

import jax
import jax.numpy as jnp
from jax import lax
from jax.experimental import pallas as pl
from jax.experimental.pallas import tpu as pltpu

F32 = jnp.float32
BF16 = jnp.bfloat16

EPS = 1e-6
LRU_C = 8.0
CONV_WIDTH = 4
LRU_HEADS = 8
LRU_GROUP = 256
HGRN_HEAD_DIM = 128
HGRN_CHUNK = 64
HGRN_SUB = 16
HGRN_STAGE_CHUNKS = 2
SGU_CHUNK = 128
SGU_GROUPS = 4
POOL_WINDOWS = (2, 4, 8, 16)
POOL_HALO = 16
SUBLANES = 8

MIX_BLOCK_T = 512
ODD_BLOCK_T = 1024
FFN_BLOCK_M = 1024
FFN_CHUNK = 1024
MXU_PIECE = 256
VMEM_LIMIT_BYTES = 56 * 1024 * 1024


def _dot(a, b):
    return jnp.dot(a, b, preferred_element_type=F32)


def _dot_nt(a, b):
    return lax.dot_general(a, b, (((1,), (1,)), ((), ())), preferred_element_type=F32)


def _dot_tn(a, b):
    return lax.dot_general(a, b, (((0,), (0,)), ((), ())), preferred_element_type=F32)


def _rms_norm(x, gain):
    ms = jnp.mean(x * x, axis=-1, keepdims=True)
    return x * lax.rsqrt(ms + EPS) * gain


def _sigmoid(x):
    return 0.5 * jnp.tanh(0.5 * x) + 0.5


def _sigmoid_small_tail(x):
    e = jnp.exp(-jnp.abs(x))
    big = 1.0 / (1.0 + e)
    return jnp.where(x >= 0, big, e * big)


def _silu(x):
    hx = 0.5 * x
    return hx * jnp.tanh(hx) + hx


def _gelu_tanh(x):
    c = 0.7978845608028654
    return 0.5 * x * (1.0 + jnp.tanh(c * (x + 0.044715 * (x * x * x))))


def _split2_bf16(x):
    hi = x.astype(BF16)
    lo = (x - hi.astype(F32)).astype(BF16)
    return hi, lo


def _const_spec(shape):
    nd = len(shape)
    return pl.BlockSpec(shape, lambda *_: (0,) * nd, pipeline_mode=pl.Buffered(1))


def _layer_spec(stacked, layer):
    tail = stacked.shape[1:]
    return pl.BlockSpec((None,) + tail, lambda *_: (layer,) + (0,) * len(tail), pipeline_mode=pl.Buffered(1))


def _ffn_kernel(h_ref, gain_ref, w1_ref, w2_ref, gfin_ref, o_ref, *, final_norm):
    h = h_ref[...]
    hn = _rms_norm(h, gain_ref[...]).astype(BF16)
    acc = h
    d_ff = w1_ref.shape[1]
    for c in range(d_ff // FFN_CHUNK):
        cols = slice(c * FFN_CHUNK, (c + 1) * FFN_CHUNK)
        hid = _dot(hn, w1_ref[:, cols])
        hid = jnp.square(jnp.maximum(hid, 0.0)).astype(BF16)
        acc = acc + _dot(hid, w2_ref[cols, :])
    if final_norm:
        acc = _rms_norm(acc, gfin_ref[...])
    o_ref[...] = acc


def _ffn_call(h2d, gain, w1_all, w2_all, layer, gfin, final_norm):
    m, d = h2d.shape
    return pl.pallas_call(
        functools.partial(_ffn_kernel, final_norm=final_norm),
        grid=(m // FFN_BLOCK_M,),
        in_specs=[
            pl.BlockSpec((FFN_BLOCK_M, d), lambda i: (i, 0)),
            _const_spec((1, d)),
            _layer_spec(w1_all, layer),
            _layer_spec(w2_all, layer),
            _const_spec((1, d)),
        ],
        out_specs=pl.BlockSpec((FFN_BLOCK_M, d), lambda i: (i, 0)),
        out_shape=jax.ShapeDtypeStruct((m, d), F32),
        compiler_params=pltpu.CompilerParams(
            dimension_semantics=("parallel",), vmem_limit_bytes=VMEM_LIMIT_BYTES),
        name="ffn_final" if final_norm else "ffn",
    )(h2d, gain, w1_all, w2_all, gfin)


def _pipeline_maps(n_blk, n_t):
    def in_map(g):
        blk = jnp.minimum(g, n_blk - 1)
        return (blk // n_t, blk % n_t, 0)

    def out_map(g):
        blk = jnp.maximum(g - 1, 0)
        return (blk // n_t, blk % n_t, 0)

    return in_map, out_map


def _projection_pieces(xn, w_in_ref, pnew):
    def make(c0):
        def piece():
            pnew[:, c0:c0 + MXU_PIECE] = _dot(xn, w_in_ref[:, c0:c0 + MXU_PIECE])
        return piece

    return [make(c0) for c0 in range(0, w_in_ref.shape[1], MXU_PIECE)]


def _lru_scan(a, b, carry):
    t, w = a.shape
    g = t // SUBLANES
    a3 = a.reshape(g, SUBLANES, w)
    b3 = b.reshape(g, SUBLANES, w)
    row = lax.broadcasted_iota(jnp.int32, (g, SUBLANES, w), 1)
    sh = 1
    while sh < SUBLANES:
        keep = row >= sh
        a_prev = jnp.where(keep, pltpu.roll(a3, sh, 1), 1.0)
        b_prev = jnp.where(keep, pltpu.roll(b3, sh, 1), 0.0)
        b3 = a3 * b_prev + b3
        a3 = a3 * a_prev
        sh *= 2

    def chain():
        c = carry
        outs = []
        for i in range(g):
            hblk = a3[i] * c + b3[i]
            outs.append(hblk)
            c = hblk[SUBLANES - 1:SUBLANES, :]
        return jnp.concatenate(outs, axis=0), c

    return chain


def _block_diag2(a, b):
    za = jnp.zeros(a.shape, a.dtype)
    zb = jnp.zeros(b.shape, b.dtype)
    return jnp.concatenate([jnp.concatenate([a, zb], axis=1), jnp.concatenate([za, b], axis=1)], axis=0)


def _even_kernel(h_ref, vec_ref, w_in_ref, wg_ref, w_out_ref, o_ref,
                 pcur, pnew, lru_carry, state_t, mix, hres, *, layer_j, n_t, lw, hw, n_lb):
    t_blk = h_ref.shape[0]
    gain = vec_ref[0:1, :]
    conv_b, b_a = vec_ref[1:2, 0:lw], vec_ref[1:2, lw:2 * lw]
    b_x, lam = vec_ref[2:3, 0:lw], vec_ref[2:3, lw:2 * lw]
    hnorm = vec_ref[3:4, 0:hw]
    conv_w = vec_ref[4:4 + CONV_WIDTH, 0:lw]
    logits = vec_ref[4 + CONV_WIDTH:4 + CONV_WIDTH + n_lb, 0:hw]
    n_heads = hw // HGRN_HEAD_DIM
    hdr = SUBLANES
    g = pl.program_id(0)
    blk_r = jnp.maximum(g - 1, 0)

    @pl.when(g == 0)
    def _():
        pcur[...] = jnp.zeros(pcur.shape, F32)
        hres[...] = jnp.zeros(hres.shape, F32)

    @pl.when(blk_r % n_t == 0)
    def _():
        lru_carry[...] = jnp.zeros_like(lru_carry)
        state_t[...] = jnp.zeros_like(state_t)

    xn = _rms_norm(h_ref[...], gain).astype(BF16)
    fillers = _projection_pieces(xn, w_in_ref, pnew)

    def fill():
        if fillers:
            fillers.pop(0)()

    pr = pcur
    fill()
    fill()
    fill()

    neg_lam = -lam
    softplus = jnp.maximum(neg_lam, 0.0) + jnp.log1p(jnp.exp(-jnp.abs(neg_lam)))
    log_a_scale = -LRU_C * softplus
    for gi in range(lw // LRU_GROUP):
        ln = slice(gi * LRU_GROUP, (gi + 1) * LRU_GROUP)
        ua = conv_b[:, ln] + conv_w[CONV_WIDTH - 1:CONV_WIDTH, ln] * pr[hdr:, ln]
        for k in range(CONV_WIDTH - 1):
            r0 = hdr - (CONV_WIDTH - 1) + k
            ua = ua + conv_w[k:k + 1, ln] * pr[r0:r0 + t_blk, ln]
        gates = _dot(ua.astype(BF16), wg_ref[gi])
        fill()
        r = _sigmoid(gates[:, :LRU_GROUP] + b_a[:, ln])
        i_gate = _sigmoid(gates[:, LRU_GROUP:] + b_x[:, ln])
        log_a = log_a_scale[:, ln] * r
        a = jnp.exp(log_a)
        th = jnp.tanh(log_a)
        y2 = -2.0 * th / (1.0 - th)
        beta = jnp.where(y2 > 0.0, y2 * lax.rsqrt(y2), 0.0)
        b_term = beta * (i_gate * ua)
        chain = _lru_scan(a, b_term, lru_carry[0:1, ln])
        fill()
        hl, carry = chain()
        lru_carry[0:1, ln] = carry
        fill()
        mix[:, ln] = (hl * _gelu_tanh(pr[hdr:, lw + gi * LRU_GROUP:lw + (gi + 1) * LRU_GROUP])).astype(BF16)
        if gi == 0:
            fill()

    while len(fillers) > 2:
        fill()

    mx = logits[0:1, :]
    for i in range(1, n_lb):
        mx = jnp.maximum(mx, logits[i:i + 1, :])
    es = [jnp.exp(logits[i:i + 1, :] - mx) for i in range(n_lb)]
    den = es[0]
    for e in es[1:]:
        den = den + e
    ps = [e / den for e in es]
    cum = ps[0]
    for p in ps[1:layer_j + 1]:
        cum = cum + p
    lb = cum - ps[0]
    one_m_lb = 1.0 - lb

    c = HGRN_CHUNK
    d = HGRN_HEAD_DIM
    n_sub = c // HGRN_SUB
    n_chunks = t_blk // c
    ri = lax.broadcasted_iota(jnp.int32, (c, c), 0)
    ci = lax.broadcasted_iota(jnp.int32, (c, c), 1)
    tri = (ci <= ri).astype(BF16)
    rp = lax.broadcasted_iota(jnp.int32, (c, 2 * c), 0)
    cp = lax.broadcasted_iota(jnp.int32, (c, 2 * c), 1) % c
    diag_mask = (cp <= rp) & ((rp // HGRN_SUB) == (cp // HGRN_SUB))
    zeros_sub = jnp.zeros((HGRN_SUB, d), BF16)
    base = 2 * lw

    def stage_gates(ck):
        r0 = hdr + ck * c
        f_gate = lb + one_m_lb * _sigmoid_small_tail(pr[r0:r0 + c, base + hw:base + 2 * hw])
        hi, lo = _split2_bf16(jnp.log(f_gate))
        bc = _dot(tri, jnp.concatenate([hi, lo], axis=1))
        return 1.0 - f_gate, bc

    def stage_scores(ck, hd, k_all, bc):
        r0 = hdr + ck * c
        ln = slice(hd * d, (hd + 1) * d)
        c_q = base + hd * d
        bcum = bc[:, ln] + bc[:, hw + hd * d:hw + (hd + 1) * d]
        kk = k_all[:, ln]
        qf = _silu(pr[r0:r0 + c, c_q:c_q + d])
        vv = pr[r0:r0 + c, c_q + 2 * hw:c_q + 2 * hw + d].astype(BF16)
        m = [jnp.zeros((1, d), F32)] + [bcum[(i + 1) * HGRN_SUB - 1:(i + 1) * HGRN_SUB, :] for i in range(n_sub)]
        start = jnp.concatenate([jnp.broadcast_to(m[i], (HGRN_SUB, d)) for i in range(n_sub)], axis=0)
        dec = jnp.exp(bcum - start)
        q_d = qf * dec
        k_d = kk * (1.0 / dec)
        q_sub = [q_d[i * HGRN_SUB:(i + 1) * HGRN_SUB, :] for i in range(n_sub)]
        k_sub = [k_d[i * HGRN_SUB:(i + 1) * HGRN_SUB, :] * jnp.exp(m[i + 1] - m[i]) for i in range(n_sub)]
        q_inter = jnp.concatenate([(q_sub[i] * jnp.exp(m[i])).astype(BF16) for i in range(n_sub)], axis=0)
        k_dec = jnp.concatenate(
            [(k_sub[i] * jnp.exp(m[n_sub] - m[i + 1])).astype(BF16) for i in range(n_sub)], axis=0)
        q_off = jnp.concatenate([
            jnp.concatenate([zeros_sub if i <= j else (q_sub[i] * jnp.exp(m[i] - m[j + 1])).astype(BF16)
                             for i in range(n_sub)], axis=0)
            for j in range(n_sub - 1)], axis=1)
        k_off = jnp.concatenate([
            jnp.concatenate([k_sub[i].astype(BF16) if i == j else zeros_sub for i in range(n_sub)], axis=0)
            for j in range(n_sub - 1)], axis=1)
        return q_off, k_off, q_d.astype(BF16), k_d.astype(BF16), k_dec, q_inter, vv, jnp.exp(m[n_sub])

    def stage_attn(sa, sb):
        q_off_a, k_off_a, q_d_a, k_d_a, k_dec_a, q_inter_a, vv_a, e_last_a = sa
        q_off_b, k_off_b, q_d_b, k_d_b, k_dec_b, q_inter_b, vv_b, e_last_b = sb
        a_off = _dot_nt(jnp.concatenate([q_off_a, q_off_b], axis=1), _block_diag2(k_off_a, k_off_b))
        a_diag = _dot_nt(jnp.concatenate([q_d_a, q_d_b], axis=1), _block_diag2(k_d_a, k_d_b))
        upd_a = _dot_tn(vv_a, k_dec_a)
        upd_b = _dot_tn(vv_b, k_dec_b)
        q_inter = jnp.concatenate([q_inter_a, q_inter_b], axis=1)
        return a_off, a_diag, upd_a, upd_b, q_inter, _block_diag2(vv_a, vv_b), e_last_a, e_last_b

    def stage_output(at, st_a, st_b):
        a_off, a_diag, upd_a, upd_b, q_inter, vv2, e_last_a, e_last_b = at
        attn = a_off + jnp.where(diag_mask, a_diag, 0.0)
        o = _dot_nt(q_inter, _block_diag2(st_a.astype(BF16), st_b.astype(BF16)))
        o = o + _dot(attn.astype(BF16), vv2)
        return o[:, 0:d], o[:, d:2 * d], st_a * e_last_a + upd_a, st_b * e_last_b + upd_b

    def stage_store(ck, hd, o):
        r0 = hdr + ck * c
        c_g = base + 3 * hw + hd * d
        o = o * lax.rsqrt(jnp.mean(o * o, axis=-1, keepdims=True) + EPS)
        yb = o * hnorm[:, hd * d:(hd + 1) * d] * _silu(pr[r0:r0 + c, c_g:c_g + d])
        mix[ck * c:(ck + 1) * c, lw + hd * d:lw + (hd + 1) * d] = yb.astype(BF16)

    states = [state_t[hd] for hd in range(n_heads)]
    while fillers:
        fill()
    for c0 in range(0, n_chunks, HGRN_STAGE_CHUNKS):
        cks = range(c0, c0 + HGRN_STAGE_CHUNKS)
        gates_all = {ck: stage_gates(ck) for ck in cks}
        attn_all = {ck: [stage_attn(stage_scores(ck, hd, *gates_all[ck]), stage_scores(ck, hd + 1, *gates_all[ck]))
                         for hd in range(0, n_heads, 2)] for ck in cks}
        outs_all = {}
        for ck in cks:
            outs = []
            for pi, hd in enumerate(range(0, n_heads, 2)):
                o_a, o_b, states[hd], states[hd + 1] = stage_output(attn_all[ck][pi], states[hd], states[hd + 1])
                outs += [o_a, o_b]
            outs_all[ck] = outs
        for ck in cks:
            for hd in range(n_heads):
                stage_store(ck, hd, outs_all[ck][hd])
    for hd in range(n_heads):
        state_t[hd] = states[hd]

    o_ref[...] = hres[...] + _dot(mix[...], w_out_ref[...])
    pcur[0:hdr, 0:lw] = jnp.where(g % n_t == 0, 0.0, pcur[t_blk:t_blk + hdr, 0:lw])
    pcur[hdr:, :] = pnew[...]
    hres[...] = h_ref[...]


def _pack_even_vectors(gain, conv_w, conv_b, ba, bx, lam, lb_logits, hnorm):
    d = gain.shape[-1]
    lw = conv_b.shape[-1]
    assert d == 2 * lw and hnorm.shape[-1] == lw
    left = jnp.concatenate([conv_b[None], bx[None], hnorm[None], conv_w, lb_logits], axis=0)
    right = jnp.zeros_like(left).at[0].set(ba).at[1].set(lam)
    rows = jnp.concatenate([gain[None], jnp.concatenate([left, right], axis=1)], axis=0)
    pad = -rows.shape[0] % SUBLANES
    return jnp.concatenate([rows, jnp.zeros((pad, d), rows.dtype)], axis=0)


def _even_call(h, vecs, w_in, wg, w_out, layer_j, lw, hw, n_lb):
    bsz, s, d = h.shape
    t = MIX_BLOCK_T
    n_t = s // t
    n_blk = bsz * n_t
    in_map, out_map = _pipeline_maps(n_blk, n_t)
    return pl.pallas_call(
        functools.partial(_even_kernel, layer_j=layer_j, n_t=n_t, lw=lw, hw=hw, n_lb=n_lb),
        grid=(n_blk + 1,),
        in_specs=[
            pl.BlockSpec((None, t, d), in_map),
            _const_spec(vecs.shape),
            _layer_spec(w_in, layer_j),
            _const_spec(wg.shape),
            _layer_spec(w_out, layer_j),
        ],
        out_specs=pl.BlockSpec((None, t, d), out_map),
        out_shape=jax.ShapeDtypeStruct((bsz, s, d), F32),
        scratch_shapes=[
            pltpu.VMEM((t + SUBLANES, w_in.shape[-1]), F32),
            pltpu.VMEM((t, w_in.shape[-1]), F32),
            pltpu.VMEM((SUBLANES, lw), F32),
            pltpu.VMEM((hw // HGRN_HEAD_DIM, HGRN_HEAD_DIM, HGRN_HEAD_DIM), F32),
            pltpu.VMEM((t, lw + hw), BF16),
            pltpu.VMEM((t, d), F32),
        ],
        compiler_params=pltpu.CompilerParams(
            dimension_semantics=("arbitrary",), vmem_limit_bytes=VMEM_LIMIT_BYTES),
        name="mix_even",
    )(h, vecs, w_in, wg, w_out)


def _odd_kernel(h_ref, gain_ref, w_in_ref, sgu_w_ref, sgu_b_ref, pool_w_ref, pool_scale_ref,
                w_out_ref, o_ref, pcur, pnew, mix, hres, *, n_t):
    t_blk = h_ref.shape[0]
    pw = pool_scale_ref.shape[1]
    sw = (w_in_ref.shape[1] - pw) // 2
    gdim = sw // SGU_GROUPS
    pdim = pw // len(POOL_WINDOWS)
    hdr = POOL_HALO
    g = pl.program_id(0)
    blk_r = jnp.maximum(g - 1, 0)

    @pl.when(g == 0)
    def _():
        pcur[...] = jnp.zeros(pcur.shape, F32)
        hres[...] = jnp.zeros(hres.shape, F32)

    xn = _rms_norm(h_ref[...], gain_ref[...]).astype(BF16)
    fillers = _projection_pieces(xn, w_in_ref, pnew)

    def fill():
        if fillers:
            fillers.pop(0)()

    fill()
    fill()
    fill()

    ri = lax.broadcasted_iota(jnp.int32, (SGU_CHUNK, SGU_CHUNK), 0)
    ci = lax.broadcasted_iota(jnp.int32, (SGU_CHUNK, SGU_CHUNK), 1)
    causal = ci <= ri
    for gr in range(SGU_GROUPS):
        ln = slice(gr * gdim, (gr + 1) * gdim)
        u = _gelu_tanh(pcur[hdr:, gr * gdim:(gr + 1) * gdim])
        vg = _gelu_tanh(pcur[hdr:, sw + gr * gdim:sw + (gr + 1) * gdim])
        mu = jnp.mean(vg, axis=-1, keepdims=True)
        cen = vg - mu
        var = jnp.mean(cen * cen, axis=-1, keepdims=True)
        vn = (cen * lax.rsqrt(var + EPS)).astype(BF16)
        wm = jnp.where(causal, sgu_w_ref[gr], 0.0).astype(BF16)
        bias = sgu_b_ref[gr]
        for n in range(t_blk // SGU_CHUNK):
            rows = slice(n * SGU_CHUNK, (n + 1) * SGU_CHUNK)
            sv = _dot(wm, vn[rows, :]) + bias
            mix[rows, ln] = (u[rows, :] * sv).astype(BF16)
        fill()

    pos = (blk_r % n_t) * t_blk + lax.broadcasted_iota(jnp.int32, (t_blk, 1), 0) + 1
    for gi, win in enumerate(POOL_WINDOWS):
        ln = slice(gi * pdim, (gi + 1) * pdim)
        ext = pcur[:, 2 * sw + gi * pdim:2 * sw + (gi + 1) * pdim]
        acc = ext
        sh = 1
        while sh < win:
            acc = acc + pltpu.roll(acc, sh, 0)
            sh *= 2
        inv_count = 1.0 / jnp.minimum(pos, win).astype(F32)
        pooled = acc[hdr:, :] * inv_count - ext[hdr:, :]
        yd = _dot(pooled.astype(BF16), pool_w_ref[gi]) * pool_scale_ref[:, ln]
        mix[:, sw + gi * pdim:sw + (gi + 1) * pdim] = yd.astype(BF16)
    while fillers:
        fill()

    o_ref[...] = hres[...] + _dot(mix[...], w_out_ref[...])
    pcur[0:hdr, 2 * sw:] = jnp.where(g % n_t == 0, 0.0, pcur[t_blk:t_blk + hdr, 2 * sw:])
    pcur[hdr:, :] = pnew[...]
    hres[...] = h_ref[...]


def _odd_call(h, gain, w_in, sgu_w, sgu_b, pool_w, pool_scale, w_out, layer_j):
    bsz, s, d = h.shape
    mw = w_out.shape[-2]
    t = ODD_BLOCK_T
    n_t = s // t
    n_blk = bsz * n_t
    in_map, out_map = _pipeline_maps(n_blk, n_t)
    return pl.pallas_call(
        functools.partial(_odd_kernel, n_t=n_t),
        grid=(n_blk + 1,),
        in_specs=[
            pl.BlockSpec((None, t, d), in_map),
            _const_spec(gain.shape),
            _layer_spec(w_in, layer_j),
            _const_spec(sgu_w.shape),
            _const_spec(sgu_b.shape),
            _layer_spec(pool_w, layer_j),
            _const_spec(pool_scale.shape),
            _layer_spec(w_out, layer_j),
        ],
        out_specs=pl.BlockSpec((None, t, d), out_map),
        out_shape=jax.ShapeDtypeStruct((bsz, s, d), F32),
        scratch_shapes=[
            pltpu.VMEM((t + POOL_HALO, w_in.shape[-1]), F32),
            pltpu.VMEM((t, w_in.shape[-1]), F32),
            pltpu.VMEM((t, mw), BF16),
            pltpu.VMEM((t, d), F32),
        ],
        compiler_params=pltpu.CompilerParams(
            dimension_semantics=("arbitrary",), vmem_limit_bytes=VMEM_LIMIT_BYTES),
        name="mix_odd",
    )(h, gain, w_in, sgu_w, sgu_b, pool_w, pool_scale, w_out)


def _gate_block_diag(wa, wx):
    n_heads, hd, _ = wa.shape
    per = LRU_GROUP // hd
    groups = n_heads // per
    out = jnp.zeros((groups, LRU_GROUP, 2 * LRU_GROUP), wa.dtype)
    for hh in range(n_heads):
        g, p = divmod(hh, per)
        out = out.at[g, p * hd:(p + 1) * hd, p * hd:(p + 1) * hd].set(wa[hh])
        out = out.at[g, p * hd:(p + 1) * hd, LRU_GROUP + p * hd:LRU_GROUP + (p + 1) * hd].set(wx[hh])
    return out


def kernel(x, norm_mix, norm_ffn, w_in_even, conv_w, conv_b, lru_wa, lru_ba, lru_wx, lru_bx, lru_lambda, hgrn_lb_logits, hgrn_norm, w_out_even, w_in_odd, sgu_w, sgu_b, pool_w, pool_scale, w_out_odd, w_ffn_in, w_ffn_out, norm_final):
    bsz, s, d = x.shape
    depth = norm_mix.shape[0]
    assert s % MIX_BLOCK_T == 0 and s % ODD_BLOCK_T == 0 and (bsz * s) % FFN_BLOCK_M == 0
    assert ODD_BLOCK_T % SGU_CHUNK == 0 and MIX_BLOCK_T % HGRN_CHUNK == 0
    row = lambda a: a.reshape(1, -1)
    w_in_even, w_out_even, w_in_odd, w_out_odd, pool_w, w_ffn_in, w_ffn_out = (
        w.astype(BF16) for w in (w_in_even, w_out_even, w_in_odd, w_out_odd, pool_w, w_ffn_in, w_ffn_out))
    h = x
    for layer in range(depth):
        j = layer // 2
        if layer % 2 == 0:
            wg = _gate_block_diag(lru_wa[j], lru_wx[j]).astype(BF16)
            vecs = _pack_even_vectors(norm_mix[layer], conv_w[j], conv_b[j], lru_ba[j], lru_bx[j], lru_lambda[j],
                                      hgrn_lb_logits, hgrn_norm[j])
            h = _even_call(h, vecs, w_in_even, wg, w_out_even, j, conv_b.shape[-1], hgrn_norm.shape[-1],
                           hgrn_lb_logits.shape[0])
        else:
            h = _odd_call(h, row(norm_mix[layer]), w_in_odd, sgu_w[j], sgu_b[j][..., None], pool_w,
                          row(pool_scale[j]), w_out_odd, j)
        h = _ffn_call(h.reshape(bsz * s, d), row(norm_ffn[layer]), w_ffn_in, w_ffn_out, layer, row(norm_final),
                      final_norm=(layer == depth - 1)).reshape(bsz, s, d)
    return h
```

```python
import functools

import jax
import jax.numpy as jnp
from jax import lax
from jax.experimental import pallas as pl
from jax.experimental.pallas import tpu as pltpu

F32 = jnp.float32
BF16 = jnp.bfloat16

EPS = 1e-6
LRU_C = 8.0
CONV_WIDTH = 4
LRU_HEADS = 8
LRU_GROUP = 256
HGRN_HEAD_DIM = 128
HGRN_CHUNK = 64
HGRN_SUB = 16
HGRN_STAGE_CHUNKS = 2
SGU_CHUNK = 128
SGU_GROUPS = 4
POOL_WINDOWS = (2, 4, 8, 16)
POOL_HALO = 16
SUBLANES = 8

MIX_BLOCK_T = 512
ODD_BLOCK_T = 1024
FFN_BLOCK_M = 1024
FFN_CHUNK = 1024
MXU_PIECE = 256
VMEM_LIMIT_BYTES = 56 * 1024 * 1024


def _dot(a, b):
    return jnp.dot(a, b, preferred_element_type=F32)


def _dot_nt(a, b):
    return lax.dot_general(a, b, (((1,), (1,)), ((), ())), preferred_element_type=F32)


def _dot_tn(a, b):
    return lax.dot_general(a, b, (((0,), (0,)), ((), ())), preferred_element_type=F32)


def _rms_norm(x, gain):
    ms = jnp.mean(x * x, axis=-1, keepdims=True)
    return x * lax.rsqrt(ms + EPS) * gain


def _sigmoid(x):
    return 0.5 * jnp.tanh(0.5 * x) + 0.5


def _sigmoid_small_tail(x):
    e = jnp.exp(-jnp.abs(x))
    big = 1.0 / (1.0 + e)
    return jnp.where(x >= 0, big, e * big)


def _silu(x):
    hx = 0.5 * x
    return hx * jnp.tanh(hx) + hx


def _gelu_tanh(x):
    c = 0.7978845608028654
    return 0.5 * x * (1.0 + jnp.tanh(c * (x + 0.044715 * (x * x * x))))


def _split2_bf16(x):
    hi = x.astype(BF16)
    lo = (x - hi.astype(F32)).astype(BF16)
    return hi, lo


def _const_spec(shape):
    nd = len(shape)
    return pl.BlockSpec(shape, lambda *_: (0,) * nd, pipeline_mode=pl.Buffered(1))


def _layer_spec(stacked, layer):
    tail = stacked.shape[1:]
    return pl.BlockSpec((None,) + tail, lambda *_: (layer,) + (0,) * len(tail), pipeline_mode=pl.Buffered(1))


def _ffn_kernel(h_ref, gain_ref, w1_ref, w2_ref, gfin_ref, o_ref, *, final_norm):
    h = h_ref[...]
    hn = _rms_norm(h, gain_ref[...]).astype(BF16)
    acc = h
    d_ff = w1_ref.shape[1]
    for c in range(d_ff // FFN_CHUNK):
        cols = slice(c * FFN_CHUNK, (c + 1) * FFN_CHUNK)
        hid = _dot(hn, w1_ref[:, cols])
        hid = jnp.square(jnp.maximum(hid, 0.0)).astype(BF16)
        acc = acc + _dot(hid, w2_ref[cols, :])
    if final_norm:
        acc = _rms_norm(acc, gfin_ref[...])
    o_ref[...] = acc


def _ffn_call(h2d, gain, w1_all, w2_all, layer, gfin, final_norm):
    m, d = h2d.shape
    return pl.pallas_call(
        functools.partial(_ffn_kernel, final_norm=final_norm),
        grid=(m // FFN_BLOCK_M,),
        in_specs=[
            pl.BlockSpec((FFN_BLOCK_M, d), lambda i: (i, 0)),
            _const_spec((1, d)),
            _layer_spec(w1_all, layer),
            _layer_spec(w2_all, layer),
            _const_spec((1, d)),
        ],
        out_specs=pl.BlockSpec((FFN_BLOCK_M, d), lambda i: (i, 0)),
        out_shape=jax.ShapeDtypeStruct((m, d), F32),
        compiler_params=pltpu.CompilerParams(
            dimension_semantics=("parallel",), vmem_limit_bytes=VMEM_LIMIT_BYTES),
        name="ffn_final" if final_norm else "ffn",
    )(h2d, gain, w1_all, w2_all, gfin)


def _pipeline_maps(n_blk, n_t):
    def in_map(g):
        blk = jnp.minimum(g, n_blk - 1)
        return (blk // n_t, blk % n_t, 0)

    def out_map(g):
        blk = jnp.maximum(g - 1, 0)
        return (blk // n_t, blk % n_t, 0)

    return in_map, out_map


def _projection_pieces(xn, w_in_ref, pnew):
    def make(c0):
        def piece():
            pnew[:, c0:c0 + MXU_PIECE] = _dot(xn, w_in_ref[:, c0:c0 + MXU_PIECE])
        return piece

    return [make(c0) for c0 in range(0, w_in_ref.shape[1], MXU_PIECE)]


def _lru_scan(a, b, carry):
    t, w = a.shape
    g = t // SUBLANES
    a3 = a.reshape(g, SUBLANES, w)
    b3 = b.reshape(g, SUBLANES, w)
    row = lax.broadcasted_iota(jnp.int32, (g, SUBLANES, w), 1)
    sh = 1
    while sh < SUBLANES:
        keep = row >= sh
        a_prev = jnp.where(keep, pltpu.roll(a3, sh, 1), 1.0)
        b_prev = jnp.where(keep, pltpu.roll(b3, sh, 1), 0.0)
        b3 = a3 * b_prev + b3
        a3 = a3 * a_prev
        sh *= 2

    def chain():
        c = carry
        outs = []
        for i in range(g):
            hblk = a3[i] * c + b3[i]
            outs.append(hblk)
            c = hblk[SUBLANES - 1:SUBLANES, :]
        return jnp.concatenate(outs, axis=0), c

    return chain


def _block_diag2(a, b):
    za = jnp.zeros(a.shape, a.dtype)
    zb = jnp.zeros(b.shape, b.dtype)
    return jnp.concatenate([jnp.concatenate([a, zb], axis=1), jnp.concatenate([za, b], axis=1)], axis=0)


def _even_kernel(h_ref, hprev_ref, vec_ref, w_in_ref, wg_ref, w_out_ref, o_ref,
                 pcur, pnew, lru_carry, state_t, mix, *, layer_j, n_t, lw, hw, n_lb):
    t_blk = h_ref.shape[0]
    gain = vec_ref[0:1, :]
    conv_b, b_a = vec_ref[1:2, 0:lw], vec_ref[1:2, lw:2 * lw]
    b_x, lam = vec_ref[2:3, 0:lw], vec_ref[2:3, lw:2 * lw]
    hnorm = vec_ref[3:4, 0:hw]
    conv_w = vec_ref[4:4 + CONV_WIDTH, 0:lw]
    logits = vec_ref[4 + CONV_WIDTH:4 + CONV_WIDTH + n_lb, 0:hw]
    n_heads = hw // HGRN_HEAD_DIM
    hdr = SUBLANES
    g = pl.program_id(0)
    blk_r = jnp.maximum(g - 1, 0)

    @pl.when(g == 0)
    def _():
        pcur[...] = jnp.zeros(pcur.shape, F32)

    @pl.when(blk_r % n_t == 0)
    def _():
        lru_carry[...] = jnp.zeros_like(lru_carry)
        state_t[...] = jnp.zeros_like(state_t)

    xn = _rms_norm(h_ref[...], gain).astype(BF16)
    fillers = _projection_pieces(xn, w_in_ref, pnew)

    def fill():
        if fillers:
            fillers.pop(0)()

    pr = pcur
    fill()
    fill()
    fill()

    neg_lam = -lam
    softplus = jnp.maximum(neg_lam, 0.0) + jnp.log1p(jnp.exp(-jnp.abs(neg_lam)))
    log_a_scale = -LRU_C * softplus
    for gi in range(lw // LRU_GROUP):
        ln = slice(gi * LRU_GROUP, (gi + 1) * LRU_GROUP)
        ua = conv_b[:, ln] + conv_w[CONV_WIDTH - 1:CONV_WIDTH, ln] * pr[hdr:, ln]
        for k in range(CONV_WIDTH - 1):
            r0 = hdr - (CONV_WIDTH - 1) + k
            ua = ua + conv_w[k:k + 1, ln] * pr[r0:r0 + t_blk, ln]
        gates = _dot(ua.astype(BF16), wg_ref[gi])
        fill()
        r = _sigmoid(gates[:, :LRU_GROUP] + b_a[:, ln])
        i_gate = _sigmoid(gates[:, LRU_GROUP:] + b_x[:, ln])
        log_a = log_a_scale[:, ln] * r
        a = jnp.exp(log_a)
        th = jnp.tanh(log_a)
        y2 = -2.0 * th / (1.0 - th)
        beta = jnp.where(y2 > 0.0, y2 * lax.rsqrt(y2), 0.0)
        b_term = beta * (i_gate * ua)
        chain = _lru_scan(a, b_term, lru_carry[0:1, ln])
        fill()
        hl, carry = chain()
        lru_carry[0:1, ln] = carry
        fill()
        mix[:, ln] = (hl * _gelu_tanh(pr[hdr:, lw + gi * LRU_GROUP:lw + (gi + 1) * LRU_GROUP])).astype(BF16)
        if gi == 0:
            fill()

    while len(fillers) > 2:
        fill()

    mx = logits[0:1, :]
    for i in range(1, n_lb):
        mx = jnp.maximum(mx, logits[i:i + 1, :])
    es = [jnp.exp(logits[i:i + 1, :] - mx) for i in range(n_lb)]
    den = es[0]
    for e in es[1:]:
        den = den + e
    ps = [e / den for e in es]
    cum = ps[0]
    for p in ps[1:layer_j + 1]:
        cum = cum + p
    lb = cum - ps[0]
    one_m_lb = 1.0 - lb

    c = HGRN_CHUNK
    d = HGRN_HEAD_DIM
    n_sub = c // HGRN_SUB
    n_chunks = t_blk // c
    ri = lax.broadcasted_iota(jnp.int32, (c, c), 0)
    ci = lax.broadcasted_iota(jnp.int32, (c, c), 1)
    tri = (ci <= ri).astype(BF16)
    rp = lax.broadcasted_iota(jnp.int32, (c, 2 * c), 0)
    cp = lax.broadcasted_iota(jnp.int32, (c, 2 * c), 1) % c
    diag_mask = (cp <= rp) & ((rp // HGRN_SUB) == (cp // HGRN_SUB))
    zeros_sub = jnp.zeros((HGRN_SUB, d), BF16)
    base = 2 * lw

    def stage_gates(ck):
        r0 = hdr + ck * c
        f_gate = lb + one_m_lb * _sigmoid_small_tail(pr[r0:r0 + c, base + hw:base + 2 * hw])
        hi, lo = _split2_bf16(jnp.log(f_gate))
        bc = _dot(tri, jnp.concatenate([hi, lo], axis=1))
        return 1.0 - f_gate, bc

    def stage_scores(ck, hd, k_all, bc):
        r0 = hdr + ck * c
        ln = slice(hd * d, (hd + 1) * d)
        c_q = base + hd * d
        bcum = bc[:, ln] + bc[:, hw + hd * d:hw + (hd + 1) * d]
        kk = k_all[:, ln]
        qf = _silu(pr[r0:r0 + c, c_q:c_q + d])
        vv = pr[r0:r0 + c, c_q + 2 * hw:c_q + 2 * hw + d].astype(BF16)
        m = [jnp.zeros((1, d), F32)] + [bcum[(i + 1) * HGRN_SUB - 1:(i + 1) * HGRN_SUB, :] for i in range(n_sub)]
        start = jnp.concatenate([jnp.broadcast_to(m[i], (HGRN_SUB, d)) for i in range(n_sub)], axis=0)
        dec = jnp.exp(bcum - start)
        q_d = qf * dec
        k_d = kk * (1.0 / dec)
        q_sub = [q_d[i * HGRN_SUB:(i + 1) * HGRN_SUB, :] for i in range(n_sub)]
        k_sub = [k_d[i * HGRN_SUB:(i + 1) * HGRN_SUB, :] * jnp.exp(m[i + 1] - m[i]) for i in range(n_sub)]
        q_inter = jnp.concatenate([(q_sub[i] * jnp.exp(m[i])).astype(BF16) for i in range(n_sub)], axis=0)
        k_dec = jnp.concatenate(
            [(k_sub[i] * jnp.exp(m[n_sub] - m[i + 1])).astype(BF16) for i in range(n_sub)], axis=0)
        q_off = jnp.concatenate([
            jnp.concatenate([zeros_sub if i <= j else (q_sub[i] * jnp.exp(m[i] - m[j + 1])).astype(BF16)
                             for i in range(n_sub)], axis=0)
            for j in range(n_sub - 1)], axis=1)
        k_off = jnp.concatenate([
            jnp.concatenate([k_sub[i].astype(BF16) if i == j else zeros_sub for i in range(n_sub)], axis=0)
            for j in range(n_sub - 1)], axis=1)
        return q_off, k_off, q_d.astype(BF16), k_d.astype(BF16), k_dec, q_inter, vv, jnp.exp(m[n_sub])

    def stage_attn(sa, sb):
        q_off_a, k_off_a, q_d_a, k_d_a, k_dec_a, q_inter_a, vv_a, e_last_a = sa
        q_off_b, k_off_b, q_d_b, k_d_b, k_dec_b, q_inter_b, vv_b, e_last_b = sb
        a_off = _dot_nt(jnp.concatenate([q_off_a, q_off_b], axis=1), _block_diag2(k_off_a, k_off_b))
        a_diag = _dot_nt(jnp.concatenate([q_d_a, q_d_b], axis=1), _block_diag2(k_d_a, k_d_b))
        upd_a = _dot_tn(vv_a, k_dec_a)
        upd_b = _dot_tn(vv_b, k_dec_b)
        q_inter = jnp.concatenate([q_inter_a, q_inter_b], axis=1)
        return a_off, a_diag, upd_a, upd_b, q_inter, _block_diag2(vv_a, vv_b), e_last_a, e_last_b

    def stage_output(at, st_a, st_b):
        a_off, a_diag, upd_a, upd_b, q_inter, vv2, e_last_a, e_last_b = at
        attn = a_off + jnp.where(diag_mask, a_diag, 0.0)
        o = _dot_nt(q_inter, _block_diag2(st_a.astype(BF16), st_b.astype(BF16)))
        o = o + _dot(attn.astype(BF16), vv2)
        return o[:, 0:d], o[:, d:2 * d], st_a * e_last_a + upd_a, st_b * e_last_b + upd_b

    def stage_store(ck, hd, o):
        r0 = hdr + ck * c
        c_g = base + 3 * hw + hd * d
        o = o * lax.rsqrt(jnp.mean(o * o, axis=-1, keepdims=True) + EPS)
        yb = o * hnorm[:, hd * d:(hd + 1) * d] * _silu(pr[r0:r0 + c, c_g:c_g + d])
        mix[ck * c:(ck + 1) * c, lw + hd * d:lw + (hd + 1) * d] = yb.astype(BF16)

    states = [state_t[hd] for hd in range(n_heads)]
    while fillers:
        fill()
    for c0 in range(0, n_chunks, HGRN_STAGE_CHUNKS):
        cks = range(c0, c0 + HGRN_STAGE_CHUNKS)
        gates_all = {ck: stage_gates(ck) for ck in cks}
        attn_all = {ck: [stage_attn(stage_scores(ck, hd, *gates_all[ck]), stage_scores(ck, hd + 1, *gates_all[ck]))
                         for hd in range(0, n_heads, 2)] for ck in cks}
        outs_all = {}
        for ck in cks:
            outs = []
            for pi, hd in enumerate(range(0, n_heads, 2)):
                o_a, o_b, states[hd], states[hd + 1] = stage_output(attn_all[ck][pi], states[hd], states[hd + 1])
                outs += [o_a, o_b]
            outs_all[ck] = outs
        for ck in cks:
            for hd in range(n_heads):
                stage_store(ck, hd, outs_all[ck][hd])
    for hd in range(n_heads):
        state_t[hd] = states[hd]

    o_ref[...] = hprev_ref[...] + _dot(mix[...], w_out_ref[...])
    pcur[0:hdr, 0:lw] = jnp.where(g % n_t == 0, 0.0, pcur[t_blk:t_blk + hdr, 0:lw])
    pcur[hdr:, :] = pnew[...]


def _pack_even_vectors(gain, conv_w, conv_b, ba, bx, lam, lb_logits, hnorm):
    d = gain.shape[-1]
    lw = conv_b.shape[-1]
    assert d == 2 * lw and hnorm.shape[-1] == lw
    left = jnp.concatenate([conv_b[None], bx[None], hnorm[None], conv_w, lb_logits], axis=0)
    right = jnp.zeros_like(left).at[0].set(ba).at[1].set(lam)
    rows = jnp.concatenate([gain[None], jnp.concatenate([left, right], axis=1)], axis=0)
    pad = -rows.shape[0] % SUBLANES
    return jnp.concatenate([rows, jnp.zeros((pad, d), rows.dtype)], axis=0)


def _even_call(h, vecs, w_in, wg, w_out, layer_j, lw, hw, n_lb):
    bsz, s, d = h.shape
    t = MIX_BLOCK_T
    n_t = s // t
    n_blk = bsz * n_t
    in_map, out_map = _pipeline_maps(n_blk, n_t)
    return pl.pallas_call(
        functools.partial(_even_kernel, layer_j=layer_j, n_t=n_t, lw=lw, hw=hw, n_lb=n_lb),
        grid=(n_blk + 1,),
        in_specs=[
            pl.BlockSpec((None, t, d), in_map),
            pl.BlockSpec((None, t, d), out_map),
            _const_spec(vecs.shape),
            _layer_spec(w_in, layer_j),
            _const_spec(wg.shape),
            _layer_spec(w_out, layer_j),
        ],
        out_specs=pl.BlockSpec((None, t, d), out_map),
        out_shape=jax.ShapeDtypeStruct((bsz, s, d), F32),
        scratch_shapes=[
            pltpu.VMEM((t + SUBLANES, w_in.shape[-1]), F32),
            pltpu.VMEM((t, w_in.shape[-1]), F32),
            pltpu.VMEM((SUBLANES, lw), F32),
            pltpu.VMEM((hw // HGRN_HEAD_DIM, HGRN_HEAD_DIM, HGRN_HEAD_DIM), F32),
            pltpu.VMEM((t, lw + hw), BF16),
        ],
        compiler_params=pltpu.CompilerParams(
            dimension_semantics=("arbitrary",), vmem_limit_bytes=VMEM_LIMIT_BYTES),
        name="mix_even",
    )(h, h, vecs, w_in, wg, w_out)


def _odd_kernel(h_ref, hprev_ref, gain_ref, w_in_ref, sgu_w_ref, sgu_b_ref, pool_w_ref, pool_scale_ref,
                w_out_ref, o_ref, pcur, pnew, mix, *, n_t):
    t_blk = h_ref.shape[0]
    pw = pool_scale_ref.shape[1]
    sw = (w_in_ref.shape[1] - pw) // 2
    gdim = sw // SGU_GROUPS
    pdim = pw // len(POOL_WINDOWS)
    hdr = POOL_HALO
    g = pl.program_id(0)
    blk_r = jnp.maximum(g - 1, 0)

    @pl.when(g == 0)
    def _():
        pcur[...] = jnp.zeros(pcur.shape, F32)

    xn = _rms_norm(h_ref[...], gain_ref[...]).astype(BF16)
    fillers = _projection_pieces(xn, w_in_ref, pnew)

    def fill():
        if fillers:
            fillers.pop(0)()

    fill()
    fill()
    fill()

    ri = lax.broadcasted_iota(jnp.int32, (SGU_CHUNK, SGU_CHUNK), 0)
    ci = lax.broadcasted_iota(jnp.int32, (SGU_CHUNK, SGU_CHUNK), 1)
    causal = ci <= ri
    for gr in range(SGU_GROUPS):
        ln = slice(gr * gdim, (gr + 1) * gdim)
        u = _gelu_tanh(pcur[hdr:, gr * gdim:(gr + 1) * gdim])
        vg = _gelu_tanh(pcur[hdr:, sw + gr * gdim:sw + (gr + 1) * gdim])
        mu = jnp.mean(vg, axis=-1, keepdims=True)
        cen = vg - mu
        var = jnp.mean(cen * cen, axis=-1, keepdims=True)
        vn = (cen * lax.rsqrt(var + EPS)).astype(BF16)
        wm = jnp.where(causal, sgu_w_ref[gr], 0.0).astype(BF16)
        bias = sgu_b_ref[gr]
        for n in range(t_blk // SGU_CHUNK):
            rows = slice(n * SGU_CHUNK, (n + 1) * SGU_CHUNK)
            sv = _dot(wm, vn[rows, :]) + bias
            mix[rows, ln] = (u[rows, :] * sv).astype(BF16)
        fill()

    pos = (blk_r % n_t) * t_blk + lax.broadcasted_iota(jnp.int32, (t_blk, 1), 0) + 1
    for gi, win in enumerate(POOL_WINDOWS):
        ln = slice(gi * pdim, (gi + 1) * pdim)
        ext = pcur[:, 2 * sw + gi * pdim:2 * sw + (gi + 1) * pdim]
        acc = ext
        sh = 1
        while sh < win:
            acc = acc + pltpu.roll(acc, sh, 0)
            sh *= 2
        inv_count = 1.0 / jnp.minimum(pos, win).astype(F32)
        pooled = acc[hdr:, :] * inv_count - ext[hdr:, :]
        yd = _dot(pooled.astype(BF16), pool_w_ref[gi]) * pool_scale_ref[:, ln]
        mix[:, sw + gi * pdim:sw + (gi + 1) * pdim] = yd.astype(BF16)
    while fillers:
        fill()

    o_ref[...] = hprev_ref[...] + _dot(mix[...], w_out_ref[...])
    pcur[0:hdr, 2 * sw:] = jnp.where(g % n_t == 0, 0.0, pcur[t_blk:t_blk + hdr, 2 * sw:])
    pcur[hdr:, :] = pnew[...]


def _odd_call(h, gain, w_in, sgu_w, sgu_b, pool_w, pool_scale, w_out, layer_j):
    bsz, s, d = h.shape
    mw = w_out.shape[-2]
    t = ODD_BLOCK_T
    n_t = s // t
    n_blk = bsz * n_t
    in_map, out_map = _pipeline_maps(n_blk, n_t)
    return pl.pallas_call(
        functools.partial(_odd_kernel, n_t=n_t),
        grid=(n_blk + 1,),
        in_specs=[
            pl.BlockSpec((None, t, d), in_map),
            pl.BlockSpec((None, t, d), out_map),
            _const_spec(gain.shape),
            _layer_spec(w_in, layer_j),
            _const_spec(sgu_w.shape),
            _const_spec(sgu_b.shape),
            _layer_spec(pool_w, layer_j),
            _const_spec(pool_scale.shape),
            _layer_spec(w_out, layer_j),
        ],
        out_specs=pl.BlockSpec((None, t, d), out_map),
        out_shape=jax.ShapeDtypeStruct((bsz, s, d), F32),
        scratch_shapes=[
            pltpu.VMEM((t + POOL_HALO, w_in.shape[-1]), F32),
            pltpu.VMEM((t, w_in.shape[-1]), F32),
            pltpu.VMEM((t, mw), BF16),
        ],
        compiler_params=pltpu.CompilerParams(
            dimension_semantics=("arbitrary",), vmem_limit_bytes=VMEM_LIMIT_BYTES),
        name="mix_odd",
    )(h, h, gain, w_in, sgu_w, sgu_b, pool_w, pool_scale, w_out)


def _gate_block_diag(wa, wx):
    n_heads, hd, _ = wa.shape
    per = LRU_GROUP // hd
    groups = n_heads // per
    out = jnp.zeros((groups, LRU_GROUP, 2 * LRU_GROUP), wa.dtype)
    for hh in range(n_heads):
        g, p = divmod(hh, per)
        out = out.at[g, p * hd:(p + 1) * hd, p * hd:(p + 1) * hd].set(wa[hh])
        out = out.at[g, p * hd:(p + 1) * hd, LRU_GROUP + p * hd:LRU_GROUP + (p + 1) * hd].set(wx[hh])
    return out


def kernel(x, norm_mix, norm_ffn, w_in_even, conv_w, conv_b, lru_wa, lru_ba, lru_wx, lru_bx, lru_lambda, hgrn_lb_logits, hgrn_norm, w_out_even, w_in_odd, sgu_w, sgu_b, pool_w, pool_scale, w_out_odd, w_ffn_in, w_ffn_out, norm_final):
    bsz, s, d = x.shape
    depth = norm_mix.shape[0]
    assert s % MIX_BLOCK_T == 0 and s % ODD_BLOCK_T == 0 and (bsz * s) % FFN_BLOCK_M == 0
    assert ODD_BLOCK_T % SGU_CHUNK == 0 and MIX_BLOCK_T % HGRN_CHUNK == 0
    row = lambda a: a.reshape(1, -1)
    w_in_even, w_out_even, w_in_odd, w_out_odd, pool_w, w_ffn_in, w_ffn_out = (
        w.astype(BF16) for w in (w_in_even, w_out_even, w_in_odd, w_out_odd, pool_w, w_ffn_in, w_ffn_out))
    h = x
    for layer in range(depth):
        j = layer // 2
        if layer % 2 == 0:
            wg = _gate_block_diag(lru_wa[j], lru_wx[j]).astype(BF16)
            vecs = _pack_even_vectors(norm_mix[layer], conv_w[j], conv_b[j], lru_ba[j], lru_bx[j], lru_lambda[j],
                                      hgrn_lb_logits, hgrn_norm[j])
            h = _even_call(h, vecs, w_in_even, wg, w_out_even, j, conv_b.shape[-1], hgrn_norm.shape[-1],
                           hgrn_lb_logits.shape[0])
        else:
            h = _odd_call(h, row(norm_mix[layer]), w_in_odd, sgu_w[j], sgu_b[j][..., None], pool_w,
                          row(pool_scale[j]), w_out_odd, j)
        h = _ffn_call(h.reshape(bsz * s, d), row(norm_ffn[layer]), w_ffn_in, w_ffn_out, layer, row(norm_final),
                      final_norm=(layer == depth - 1)).reshape(bsz, s, d)
    return h
```

```python
import functools

import jax
import jax.numpy as jnp
from jax import lax
from jax.experimental import pallas as pl
from jax.experimental.pallas import tpu as pltpu

F32 = jnp.float32
BF16 = jnp.bfloat16

EPS = 1e-6
LRU_C = 8.0
CONV_WIDTH = 4
LRU_HEADS = 8
LRU_GROUP = 256
HGRN_HEAD_DIM = 128
HGRN_CHUNK = 64
HGRN_SUB = 16
HGRN_STAGE_CHUNKS = 2
SGU_CHUNK = 128
SGU_GROUPS = 4
POOL_WINDOWS = (2, 4, 8, 16)
POOL_HALO = 16
SUBLANES = 8

MIX_BLOCK_T = 512
ODD_BLOCK_T = 1024
FFN_BLOCK_M = 1024
FFN_CHUNK = 1024
MXU_PIECE = 256
VMEM_LIMIT_BYTES = 56 * 1024 * 1024


def _dot(a, b):
    return jnp.dot(a, b, preferred_element_type=F32)


def _dot_nt(a, b):
    return lax.dot_general(a, b, (((1,), (1,)), ((), ())), preferred_element_type=F32)


def _dot_tn(a, b):
    return lax.dot_general(a, b, (((0,), (0,)), ((), ())), preferred_element_type=F32)


def _rms_norm(x, gain):
    ms = jnp.mean(x * x, axis=-1, keepdims=True)
    return x * lax.rsqrt(ms + EPS) * gain


def _sigmoid(x):
    return 0.5 * jnp.tanh(0.5 * x) + 0.5


def _sigmoid_small_tail(x):
    e = jnp.exp(-jnp.abs(x))
    big = 1.0 / (1.0 + e)
    return jnp.where(x >= 0, big, e * big)


def _silu(x):
    hx = 0.5 * x
    return hx * jnp.tanh(hx) + hx


def _gelu_tanh(x):
    c = 0.7978845608028654
    return 0.5 * x * (1.0 + jnp.tanh(c * (x + 0.044715 * (x * x * x))))


def _split2_bf16(x):
    hi = x.astype(BF16)
    lo = (x - hi.astype(F32)).astype(BF16)
    return hi, lo


def _const_spec(shape):
    nd = len(shape)
    return pl.BlockSpec(shape, lambda *_: (0,) * nd, pipeline_mode=pl.Buffered(1))


def _layer_spec(stacked, layer):
    tail = stacked.shape[1:]
    return pl.BlockSpec((None,) + tail, lambda *_: (layer,) + (0,) * len(tail), pipeline_mode=pl.Buffered(1))


def _ffn_kernel(h_ref, gains_ref, w1_ref, w2_ref, o_ref, *, final_norm):
    h = h_ref[...]
    hn = _rms_norm(h, gains_ref[0:1, :]).astype(BF16)
    acc = h
    d_ff = w1_ref.shape[1]
    for c in range(d_ff // FFN_CHUNK):
        cols = slice(c * FFN_CHUNK, (c + 1) * FFN_CHUNK)
        hid = _dot(hn, w1_ref[:, cols])
        hid = jnp.square(jnp.maximum(hid, 0.0)).astype(BF16)
        acc = acc + _dot(hid, w2_ref[cols, :])
    if final_norm:
        acc = _rms_norm(acc, gains_ref[1:2, :])
    o_ref[...] = acc


def _ffn_call(h2d, gain, w1_all, w2_all, layer, gfin, final_norm):
    m, d = h2d.shape
    gains = jnp.concatenate([gain, gfin, jnp.zeros((SUBLANES - 2, d), gain.dtype)], axis=0)
    return pl.pallas_call(
        functools.partial(_ffn_kernel, final_norm=final_norm),
        grid=(m // FFN_BLOCK_M,),
        in_specs=[
            pl.BlockSpec((FFN_BLOCK_M, d), lambda i: (i, 0)),
            _const_spec(gains.shape),
            _layer_spec(w1_all, layer),
            _layer_spec(w2_all, layer),
        ],
        out_specs=pl.BlockSpec((FFN_BLOCK_M, d), lambda i: (i, 0)),
        out_shape=jax.ShapeDtypeStruct((m, d), F32),
        compiler_params=pltpu.CompilerParams(
            dimension_semantics=("parallel",), vmem_limit_bytes=VMEM_LIMIT_BYTES),
        name="ffn_final" if final_norm else "ffn",
    )(h2d, gains, w1_all, w2_all)


def _pipeline_maps(n_blk, n_t):
    def in_map(g):
        blk = jnp.minimum(g, n_blk - 1)
        return (blk // n_t, blk % n_t, 0)

    def out_map(g):
        blk = jnp.maximum(g - 1, 0)
        return (blk // n_t, blk % n_t, 0)

    return in_map, out_map


def _projection_pieces(xn, w_in_ref, pnew):
    def make(c0):
        def piece():
            pnew[:, c0:c0 + MXU_PIECE] = _dot(xn, w_in_ref[:, c0:c0 + MXU_PIECE])
        return piece

    return [make(c0) for c0 in range(0, w_in_ref.shape[1], MXU_PIECE)]


def _lru_scan(a, b, carry):
    t, w = a.shape
    g = t // SUBLANES
    a3 = a.reshape(g, SUBLANES, w)
    b3 = b.reshape(g, SUBLANES, w)
    row = lax.broadcasted_iota(jnp.int32, (g, SUBLANES, w), 1)
    sh = 1
    while sh < SUBLANES:
        keep = row >= sh
        a_prev = jnp.where(keep, pltpu.roll(a3, sh, 1), 1.0)
        b_prev = jnp.where(keep, pltpu.roll(b3, sh, 1), 0.0)
        b3 = a3 * b_prev + b3
        a3 = a3 * a_prev
        sh *= 2

    def chain():
        c = carry
        outs = []
        for i in range(g):
            hblk = a3[i] * c + b3[i]
            outs.append(hblk)
            c = hblk[SUBLANES - 1:SUBLANES, :]
        return jnp.concatenate(outs, axis=0), c

    return chain


def _block_diag2(a, b):
    za = jnp.zeros(a.shape, a.dtype)
    zb = jnp.zeros(b.shape, b.dtype)
    return jnp.concatenate([jnp.concatenate([a, zb], axis=1), jnp.concatenate([za, b], axis=1)], axis=0)


def _even_kernel(h_ref, hprev_ref, vec_ref, w_in_ref, wg_ref, w_out_ref, o_ref,
                 pcur, pnew, lru_carry, state_t, mix, *, layer_j, n_t, lw, hw, n_lb):
    t_blk = h_ref.shape[0]
    gain = vec_ref[0:1, :]
    conv_b, b_a = vec_ref[1:2, 0:lw], vec_ref[1:2, lw:2 * lw]
    b_x, lam = vec_ref[2:3, 0:lw], vec_ref[2:3, lw:2 * lw]
    hnorm = vec_ref[3:4, 0:hw]
    conv_w = vec_ref[4:4 + CONV_WIDTH, 0:lw]
    logits = vec_ref[4 + CONV_WIDTH:4 + CONV_WIDTH + n_lb, 0:hw]
    n_heads = hw // HGRN_HEAD_DIM
    hdr = SUBLANES
    g = pl.program_id(0)
    blk_r = jnp.maximum(g - 1, 0)

    @pl.when(g == 0)
    def _():
        pcur[...] = jnp.zeros(pcur.shape, F32)

    @pl.when(blk_r % n_t == 0)
    def _():
        lru_carry[...] = jnp.zeros_like(lru_carry)
        state_t[...] = jnp.zeros_like(state_t)

    xn = _rms_norm(h_ref[...], gain).astype(BF16)
    fillers = _projection_pieces(xn, w_in_ref, pnew)

    def fill():
        if fillers:
            fillers.pop(0)()

    pr = pcur
    fill()
    fill()
    fill()

    neg_lam = -lam
    softplus = jnp.maximum(neg_lam, 0.0) + jnp.log1p(jnp.exp(-jnp.abs(neg_lam)))
    log_a_scale = -LRU_C * softplus
    for gi in range(lw // LRU_GROUP):
        ln = slice(gi * LRU_GROUP, (gi + 1) * LRU_GROUP)
        ua = conv_b[:, ln] + conv_w[CONV_WIDTH - 1:CONV_WIDTH, ln] * pr[hdr:, ln]
        for k in range(CONV_WIDTH - 1):
            r0 = hdr - (CONV_WIDTH - 1) + k
            ua = ua + conv_w[k:k + 1, ln] * pr[r0:r0 + t_blk, ln]
        gates = _dot(ua.astype(BF16), wg_ref[gi])
        fill()
        r = _sigmoid(gates[:, :LRU_GROUP] + b_a[:, ln])
        i_gate = _sigmoid(gates[:, LRU_GROUP:] + b_x[:, ln])
        log_a = log_a_scale[:, ln] * r
        a = jnp.exp(log_a)
        th = jnp.tanh(log_a)
        y2 = -2.0 * th / (1.0 - th)
        beta = jnp.where(y2 > 0.0, y2 * lax.rsqrt(y2), 0.0)
        b_term = beta * (i_gate * ua)
        chain = _lru_scan(a, b_term, lru_carry[0:1, ln])
        fill()
        hl, carry = chain()
        lru_carry[0:1, ln] = carry
        fill()
        mix[:, ln] = (hl * _gelu_tanh(pr[hdr:, lw + gi * LRU_GROUP:lw + (gi + 1) * LRU_GROUP])).astype(BF16)
        if gi == 0:
            fill()

    while len(fillers) > 2:
        fill()

    mx = logits[0:1, :]
    for i in range(1, n_lb):
        mx = jnp.maximum(mx, logits[i:i + 1, :])
    es = [jnp.exp(logits[i:i + 1, :] - mx) for i in range(n_lb)]
    den = es[0]
    for e in es[1:]:
        den = den + e
    ps = [e / den for e in es]
    cum = ps[0]
    for p in ps[1:layer_j + 1]:
        cum = cum + p
    lb = cum - ps[0]
    one_m_lb = 1.0 - lb

    c = HGRN_CHUNK
    d = HGRN_HEAD_DIM
    n_sub = c // HGRN_SUB
    n_chunks = t_blk // c
    ri = lax.broadcasted_iota(jnp.int32, (c, c), 0)
    ci = lax.broadcasted_iota(jnp.int32, (c, c), 1)
    tri = (ci <= ri).astype(BF16)
    rp = lax.broadcasted_iota(jnp.int32, (c, 2 * c), 0)
    cp = lax.broadcasted_iota(jnp.int32, (c, 2 * c), 1) % c
    diag_mask = (cp <= rp) & ((rp // HGRN_SUB) == (cp // HGRN_SUB))
    zeros_sub = jnp.zeros((HGRN_SUB, d), BF16)
    base = 2 * lw

    def stage_gates(ck):
        r0 = hdr + ck * c
        f_gate = lb + one_m_lb * _sigmoid_small_tail(pr[r0:r0 + c, base + hw:base + 2 * hw])
        hi, lo = _split2_bf16(jnp.log(f_gate))
        bc = _dot(tri, jnp.concatenate([hi, lo], axis=1))
        return 1.0 - f_gate, bc

    def stage_scores(ck, hd, k_all, bc):
        r0 = hdr + ck * c
        ln = slice(hd * d, (hd + 1) * d)
        c_q = base + hd * d
        bcum = bc[:, ln] + bc[:, hw + hd * d:hw + (hd + 1) * d]
        kk = k_all[:, ln]
        qf = _silu(pr[r0:r0 + c, c_q:c_q + d])
        vv = pr[r0:r0 + c, c_q + 2 * hw:c_q + 2 * hw + d].astype(BF16)
        m = [jnp.zeros((1, d), F32)] + [bcum[(i + 1) * HGRN_SUB - 1:(i + 1) * HGRN_SUB, :] for i in range(n_sub)]
        start = jnp.concatenate([jnp.broadcast_to(m[i], (HGRN_SUB, d)) for i in range(n_sub)], axis=0)
        dec = jnp.exp(bcum - start)
        q_d = qf * dec
        k_d = kk * (1.0 / dec)
        q_sub = [q_d[i * HGRN_SUB:(i + 1) * HGRN_SUB, :] for i in range(n_sub)]
        k_sub = [k_d[i * HGRN_SUB:(i + 1) * HGRN_SUB, :] * jnp.exp(m[i + 1] - m[i]) for i in range(n_sub)]
        q_inter = jnp.concatenate([(q_sub[i] * jnp.exp(m[i])).astype(BF16) for i in range(n_sub)], axis=0)
        k_dec = jnp.concatenate(
            [(k_sub[i] * jnp.exp(m[n_sub] - m[i + 1])).astype(BF16) for i in range(n_sub)], axis=0)
        q_off = jnp.concatenate([
            jnp.concatenate([zeros_sub if i <= j else (q_sub[i] * jnp.exp(m[i] - m[j + 1])).astype(BF16)
                             for i in range(n_sub)], axis=0)
            for j in range(n_sub - 1)], axis=1)
        k_off = jnp.concatenate([
            jnp.concatenate([k_sub[i].astype(BF16) if i == j else zeros_sub for i in range(n_sub)], axis=0)
            for j in range(n_sub - 1)], axis=1)
        return q_off, k_off, q_d.astype(BF16), k_d.astype(BF16), k_dec, q_inter, vv, jnp.exp(m[n_sub])

    def stage_attn(sa, sb):
        q_off_a, k_off_a, q_d_a, k_d_a, k_dec_a, q_inter_a, vv_a, e_last_a = sa
        q_off_b, k_off_b, q_d_b, k_d_b, k_dec_b, q_inter_b, vv_b, e_last_b = sb
        a_off = _dot_nt(jnp.concatenate([q_off_a, q_off_b], axis=1), _block_diag2(k_off_a, k_off_b))
        a_diag = _dot_nt(jnp.concatenate([q_d_a, q_d_b], axis=1), _block_diag2(k_d_a, k_d_b))
        upd_a = _dot_tn(vv_a, k_dec_a)
        upd_b = _dot_tn(vv_b, k_dec_b)
        q_inter = jnp.concatenate([q_inter_a, q_inter_b], axis=1)
        return a_off, a_diag, upd_a, upd_b, q_inter, _block_diag2(vv_a, vv_b), e_last_a, e_last_b

    def stage_output(at, st_a, st_b):
        a_off, a_diag, upd_a, upd_b, q_inter, vv2, e_last_a, e_last_b = at
        attn = a_off + jnp.where(diag_mask, a_diag, 0.0)
        o = _dot_nt(q_inter, _block_diag2(st_a.astype(BF16), st_b.astype(BF16)))
        o = o + _dot(attn.astype(BF16), vv2)
        return o[:, 0:d], o[:, d:2 * d], st_a * e_last_a + upd_a, st_b * e_last_b + upd_b

    def stage_store(ck, hd, o):
        r0 = hdr + ck * c
        c_g = base + 3 * hw + hd * d
        o = o * lax.rsqrt(jnp.mean(o * o, axis=-1, keepdims=True) + EPS)
        yb = o * hnorm[:, hd * d:(hd + 1) * d] * _silu(pr[r0:r0 + c, c_g:c_g + d])
        mix[ck * c:(ck + 1) * c, lw + hd * d:lw + (hd + 1) * d] = yb.astype(BF16)

    states = [state_t[hd] for hd in range(n_heads)]
    while fillers:
        fill()
    for c0 in range(0, n_chunks, HGRN_STAGE_CHUNKS):
        cks = range(c0, c0 + HGRN_STAGE_CHUNKS)
        gates_all = {ck: stage_gates(ck) for ck in cks}
        attn_all = {ck: [stage_attn(stage_scores(ck, hd, *gates_all[ck]), stage_scores(ck, hd + 1, *gates_all[ck]))
                         for hd in range(0, n_heads, 2)] for ck in cks}
        outs_all = {}
        for ck in cks:
            outs = []
            for pi, hd in enumerate(range(0, n_heads, 2)):
                o_a, o_b, states[hd], states[hd + 1] = stage_output(attn_all[ck][pi], states[hd], states[hd + 1])
                outs += [o_a, o_b]
            outs_all[ck] = outs
        for ck in cks:
            for hd in range(n_heads):
                stage_store(ck, hd, outs_all[ck][hd])
    for hd in range(n_heads):
        state_t[hd] = states[hd]

    o_ref[...] = hprev_ref[...] + _dot(mix[...], w_out_ref[...])
    pcur[0:hdr, 0:lw] = jnp.where(g % n_t == 0, 0.0, pcur[t_blk:t_blk + hdr, 0:lw])
    pcur[hdr:, :] = pnew[...]


def _pack_even_vectors(gain, conv_w, conv_b, ba, bx, lam, lb_logits, hnorm):
    d = gain.shape[-1]
    lw = conv_b.shape[-1]
    assert d == 2 * lw and hnorm.shape[-1] == lw
    left = jnp.concatenate([conv_b[None], bx[None], hnorm[None], conv_w, lb_logits], axis=0)
    right = jnp.zeros_like(left).at[0].set(ba).at[1].set(lam)
    rows = jnp.concatenate([gain[None], jnp.concatenate([left, right], axis=1)], axis=0)
    pad = -rows.shape[0] % SUBLANES
    return jnp.concatenate([rows, jnp.zeros((pad, d), rows.dtype)], axis=0)


def _even_call(h, vecs, w_in, wg, w_out, layer_j, lw, hw, n_lb):
    bsz, s, d = h.shape
    t = MIX_BLOCK_T
    n_t = s // t
    n_blk = bsz * n_t
    in_map, out_map = _pipeline_maps(n_blk, n_t)
    return pl.pallas_call(
        functools.partial(_even_kernel, layer_j=layer_j, n_t=n_t, lw=lw, hw=hw, n_lb=n_lb),
        grid=(n_blk + 1,),
        in_specs=[
            pl.BlockSpec((None, t, d), in_map),
            pl.BlockSpec((None, t, d), out_map),
            _const_spec(vecs.shape),
            _layer_spec(w_in, layer_j),
            _const_spec(wg.shape),
            _layer_spec(w_out, layer_j),
        ],
        out_specs=pl.BlockSpec((None, t, d), out_map),
        out_shape=jax.ShapeDtypeStruct((bsz, s, d), F32),
        scratch_shapes=[
            pltpu.VMEM((t + SUBLANES, w_in.shape[-1]), F32),
            pltpu.VMEM((t, w_in.shape[-1]), F32),
            pltpu.VMEM((SUBLANES, lw), F32),
            pltpu.VMEM((hw // HGRN_HEAD_DIM, HGRN_HEAD_DIM, HGRN_HEAD_DIM), F32),
            pltpu.VMEM((t, lw + hw), BF16),
        ],
        compiler_params=pltpu.CompilerParams(
            dimension_semantics=("arbitrary",), vmem_limit_bytes=VMEM_LIMIT_BYTES),
        name="mix_even",
    )(h, h, vecs, w_in, wg, w_out)


def _odd_kernel(h_ref, hprev_ref, vec_ref, w_in_ref, sgu_w_ref, sgu_b_ref, pool_w_ref,
                w_out_ref, o_ref, pcur, pnew, mix, *, n_t, pw):
    t_blk = h_ref.shape[0]
    gain = vec_ref[0:1, :]
    pool_scale = vec_ref[1:2, 0:pw]
    sw = (w_in_ref.shape[1] - pw) // 2
    gdim = sw // SGU_GROUPS
    pdim = pw // len(POOL_WINDOWS)
    hdr = POOL_HALO
    g = pl.program_id(0)
    blk_r = jnp.maximum(g - 1, 0)

    @pl.when(g == 0)
    def _():
        pcur[...] = jnp.zeros(pcur.shape, F32)

    xn = _rms_norm(h_ref[...], gain).astype(BF16)
    fillers = _projection_pieces(xn, w_in_ref, pnew)

    def fill():
        if fillers:
            fillers.pop(0)()

    fill()
    fill()
    fill()

    ri = lax.broadcasted_iota(jnp.int32, (SGU_CHUNK, SGU_CHUNK), 0)
    ci = lax.broadcasted_iota(jnp.int32, (SGU_CHUNK, SGU_CHUNK), 1)
    causal = ci <= ri
    for gr in range(SGU_GROUPS):
        ln = slice(gr * gdim, (gr + 1) * gdim)
        u = _gelu_tanh(pcur[hdr:, gr * gdim:(gr + 1) * gdim])
        vg = _gelu_tanh(pcur[hdr:, sw + gr * gdim:sw + (gr + 1) * gdim])
        mu = jnp.mean(vg, axis=-1, keepdims=True)
        cen = vg - mu
        var = jnp.mean(cen * cen, axis=-1, keepdims=True)
        vn = (cen * lax.rsqrt(var + EPS)).astype(BF16)
        wm = jnp.where(causal, sgu_w_ref[gr], 0.0).astype(BF16)
        bias = sgu_b_ref[gr]
        for n in range(t_blk // SGU_CHUNK):
            rows = slice(n * SGU_CHUNK, (n + 1) * SGU_CHUNK)
            sv = _dot(wm, vn[rows, :]) + bias
            mix[rows, ln] = (u[rows, :] * sv).astype(BF16)
        fill()

    pos = (blk_r % n_t) * t_blk + lax.broadcasted_iota(jnp.int32, (t_blk, 1), 0) + 1
    for gi, win in enumerate(POOL_WINDOWS):
        ln = slice(gi * pdim, (gi + 1) * pdim)
        ext = pcur[:, 2 * sw + gi * pdim:2 * sw + (gi + 1) * pdim]
        acc = ext
        sh = 1
        while sh < win:
            acc = acc + pltpu.roll(acc, sh, 0)
            sh *= 2
        inv_count = 1.0 / jnp.minimum(pos, win).astype(F32)
        pooled = acc[hdr:, :] * inv_count - ext[hdr:, :]
        yd = _dot(pooled.astype(BF16), pool_w_ref[gi]) * pool_scale[:, ln]
        mix[:, sw + gi * pdim:sw + (gi + 1) * pdim] = yd.astype(BF16)
    while fillers:
        fill()

    o_ref[...] = hprev_ref[...] + _dot(mix[...], w_out_ref[...])
    pcur[0:hdr, 2 * sw:] = jnp.where(g % n_t == 0, 0.0, pcur[t_blk:t_blk + hdr, 2 * sw:])
    pcur[hdr:, :] = pnew[...]


def _odd_call(h, gain, w_in, sgu_w, sgu_b, pool_w, pool_scale, w_out, layer_j):
    bsz, s, d = h.shape
    pw = pool_scale.shape[-1]
    vecs = jnp.zeros((SUBLANES, d), gain.dtype).at[0].set(gain[0]).at[1, 0:pw].set(pool_scale[0])
    mw = w_out.shape[-2]
    t = ODD_BLOCK_T
    n_t = s // t
    n_blk = bsz * n_t
    in_map, out_map = _pipeline_maps(n_blk, n_t)
    return pl.pallas_call(
        functools.partial(_odd_kernel, n_t=n_t, pw=pw),
        grid=(n_blk + 1,),
        in_specs=[
            pl.BlockSpec((None, t, d), in_map),
            pl.BlockSpec((None, t, d), out_map),
            _const_spec(vecs.shape),
            _layer_spec(w_in, layer_j),
            _const_spec(sgu_w.shape),
            _const_spec(sgu_b.shape),
            _layer_spec(pool_w, layer_j),
            _layer_spec(w_out, layer_j),
        ],
        out_specs=pl.BlockSpec((None, t, d), out_map),
        out_shape=jax.ShapeDtypeStruct((bsz, s, d), F32),
        scratch_shapes=[
            pltpu.VMEM((t + POOL_HALO, w_in.shape[-1]), F32),
            pltpu.VMEM((t, w_in.shape[-1]), F32),
            pltpu.VMEM((t, mw), BF16),
        ],
        compiler_params=pltpu.CompilerParams(
            dimension_semantics=("arbitrary",), vmem_limit_bytes=VMEM_LIMIT_BYTES),
        name="mix_odd",
    )(h, h, vecs, w_in, sgu_w, sgu_b, pool_w, w_out)


def _gate_block_diag(wa, wx):
    n_heads, hd, _ = wa.shape
    per = LRU_GROUP // hd
    groups = n_heads // per
    out = jnp.zeros((groups, LRU_GROUP, 2 * LRU_GROUP), wa.dtype)
    for hh in range(n_heads):
        g, p = divmod(hh, per)
        out = out.at[g, p * hd:(p + 1) * hd, p * hd:(p + 1) * hd].set(wa[hh])
        out = out.at[g, p * hd:(p + 1) * hd, LRU_GROUP + p * hd:LRU_GROUP + (p + 1) * hd].set(wx[hh])
    return out


def kernel(x, norm_mix, norm_ffn, w_in_even, conv_w, conv_b, lru_wa, lru_ba, lru_wx, lru_bx, lru_lambda, hgrn_lb_logits, hgrn_norm, w_out_even, w_in_odd, sgu_w, sgu_b, pool_w, pool_scale, w_out_odd, w_ffn_in, w_ffn_out, norm_final):
    bsz, s, d = x.shape
    depth = norm_mix.shape[0]
    assert s % MIX_BLOCK_T == 0 and s % ODD_BLOCK_T == 0 and (bsz * s) % FFN_BLOCK_M == 0
    assert ODD_BLOCK_T % SGU_CHUNK == 0 and MIX_BLOCK_T % HGRN_CHUNK == 0
    row = lambda a: a.reshape(1, -1)
    w_in_even, w_out_even, w_in_odd, w_out_odd, pool_w, w_ffn_in, w_ffn_out = (
        w.astype(BF16) for w in (w_in_even, w_out_even, w_in_odd, w_out_odd, pool_w, w_ffn_in, w_ffn_out))
    h = x
    for layer in range(depth):
        j = layer // 2
        if layer % 2 == 0:
            wg = _gate_block_diag(lru_wa[j], lru_wx[j]).astype(BF16)
            vecs = _pack_even_vectors(norm_mix[layer], conv_w[j], conv_b[j], lru_ba[j], lru_bx[j], lru_lambda[j],
                                      hgrn_lb_logits, hgrn_norm[j])
            h = _even_call(h, vecs, w_in_even, wg, w_out_even, j, conv_b.shape[-1], hgrn_norm.shape[-1],
                           hgrn_lb_logits.shape[0])
        else:
            h = _odd_call(h, row(norm_mix[layer]), w_in_odd, sgu_w[j], sgu_b[j][..., None], pool_w,
                          row(pool_scale[j]), w_out_odd, j)
        h = _ffn_call(h.reshape(bsz * s, d), row(norm_ffn[layer]), w_ffn_in, w_ffn_out, layer, row(norm_final),
                      final_norm=(layer == depth - 1)).reshape(bsz, s, d)
    return h
```
